```python
import math
import jax, jax.numpy as jnp
from jax import lax
import numpy as np

D_MODEL = 1024
BATCH = 16
SEQ = 2048
DEPTH = 1
DEC_BATCH = 128
DEC_SEQ = 8
PAST_LEN = 8192
PAGE_SIZE = 128

DK = 64
QK_DIM = 2 * DK
DV = 2 * DK
N_HEADS = D_MODEL // DV
ATTN_SCALE = DK ** -0.5
Q_BLOCK = 128
N_META = 16
POOL_WINDOWS = (2, 4, 8, 16)
POOL_GROUPS = len(POOL_WINDOWS)
POOL_W = D_MODEL // 2
POOL_CG = POOL_W // POOL_GROUPS
POOL_OUT_G = D_MODEL // POOL_GROUPS
POOL_HIST = max(POOL_WINDOWS) - 1
Q_COLS = N_HEADS * QK_DIM
K_COLS = N_HEADS * QK_DIM
V_COLS = N_HEADS * DV
IN_COLS = Q_COLS + K_COLS + V_COLS + POOL_W + 2 * D_MODEL
N_GROUPS = 4
EXPERTS_PER_GROUP = 8
N_EXPERTS = N_GROUPS * EXPERTS_PER_GROUP
TOP_K_IN_GROUP = 2
D_EXPERT = D_MODEL // 2
EPS = 1e-6

kernel_name = 'hybrid_diffattn_pool_hmoe_step'


def _rmsnorm(x, g):
    xf = x.astype(jnp.float32)
    r = lax.rsqrt(jnp.mean(xf * xf, axis=-1, keepdims=True) + EPS)
    return (xf * r * g.astype(jnp.float32)).astype(x.dtype)


def _split_proj(z):
    lead = z.shape[:-1]
    o = 0
    q = z[..., o:o + Q_COLS].reshape(*lead, N_HEADS, QK_DIM); o += Q_COLS
    k = z[..., o:o + K_COLS].reshape(*lead, N_HEADS, QK_DIM); o += K_COLS
    v = z[..., o:o + V_COLS].reshape(*lead, N_HEADS, DV); o += V_COLS
    u = z[..., o:o + POOL_W]; o += POOL_W
    ga = z[..., o:o + D_MODEL]; o += D_MODEL
    gb = z[..., o:o + D_MODEL]
    return q, k, v, u, ga, gb


def _diff_lambda(lq1, lk1, lq2, lk2, lam_init):
    f = jnp.float32
    return (jnp.exp(jnp.sum(lq1.astype(f) * lk1.astype(f)))
            - jnp.exp(jnp.sum(lq2.astype(f) * lk2.astype(f))) + lam_init)


def _scores(qh, kh):
    return jnp.einsum('bqhd,bkhd->bhqk', qh, kh, preferred_element_type=jnp.float32) * ATTN_SCALE


def _prompt_diff_attn(q, k, v, lam):
    b, t = q.shape[:2]
    n_blk = -(-t // Q_BLOCK)
    tp = n_blk * Q_BLOCK
    pad = ((0, 0), (0, tp - t), (0, 0), (0, 0))
    q, k, v = jnp.pad(q, pad), jnp.pad(k, pad), jnp.pad(v, pad)
    k1, k2 = k[..., :DK], k[..., DK:]
    vf = v.astype(jnp.float32)
    kpos = jnp.arange(tp)
    qb = q.reshape(b, n_blk, Q_BLOCK, N_HEADS, QK_DIM).transpose(1, 0, 2, 3, 4)

    def block(args):
        qblk, i = args
        qpos = i * Q_BLOCK + jnp.arange(Q_BLOCK)
        mask = kpos[None, :] <= qpos[:, None]
        p1 = jax.nn.softmax(jnp.where(mask, _scores(qblk[..., :DK], k1), -jnp.inf), axis=-1)
        p2 = jax.nn.softmax(jnp.where(mask, _scores(qblk[..., DK:], k2), -jnp.inf), axis=-1)
        return jnp.einsum('bhqk,bkhd->bqhd', p1 - lam * p2, vf)

    out = lax.map(block, (qb, jnp.arange(n_blk)))
    return out.transpose(1, 0, 2, 3, 4).reshape(b, tp, N_HEADS, DV)[:, :t]


def _online_update(state, s, v):
    m, l, acc = state
    m_new = jnp.maximum(m, s.max(-1))
    corr = jnp.exp(m - m_new)
    p = jnp.exp(s - m_new[..., None])
    l = l * corr + p.sum(-1)
    acc = acc * corr[..., None] + jnp.einsum('bhqk,bkhd->bhqd', p, v.astype(jnp.float32))
    return (m_new, l, acc)


def _sample_diff_attn(q, k_new, v_new, cache_k, cache_v, page_table, lam):
    bd, s_len = q.shape[:2]
    q1, q2 = q[..., :DK], q[..., DK:]
    init = (jnp.full((bd, N_HEADS, s_len), -jnp.inf, jnp.float32),
            jnp.zeros((bd, N_HEADS, s_len), jnp.float32),
            jnp.zeros((bd, N_HEADS, s_len, DV), jnp.float32))

    def page_step(carry, pages):
        st1, st2 = carry
        kp = cache_k[pages]
        vp = cache_v[pages]
        st1 = _online_update(st1, _scores(q1, kp[..., :DK]), vp)
        st2 = _online_update(st2, _scores(q2, kp[..., DK:]), vp)
        return (st1, st2), None

    (st1, st2), _ = lax.scan(page_step, (init, init), page_table.T)
    causal = jnp.tril(jnp.ones((s_len, s_len), bool))
    st1 = _online_update(st1, jnp.where(causal, _scores(q1, k_new[..., :DK]), -jnp.inf), v_new)
    st2 = _online_update(st2, jnp.where(causal, _scores(q2, k_new[..., DK:]), -jnp.inf), v_new)
    o = st1[2] / st1[1][..., None] - lam * (st2[2] / st2[1][..., None])
    return o.transpose(0, 2, 1, 3)


def _head_out(o, subln_g, lam_init, dtype):
    r = lax.rsqrt(jnp.mean(o * o, axis=-1, keepdims=True) + EPS)
    o = o * r * subln_g.astype(jnp.float32) * (1.0 - lam_init)
    return o.reshape(*o.shape[:-2], N_HEADS * DV).astype(dtype)


def _pool_mix(u_ext, n_hist, pos0, w_pool, pool_scale):
    b, t_ext, _ = u_ext.shape
    n_new = t_ext - n_hist
    uf = u_ext.astype(jnp.float32)
    csum = jnp.concatenate([jnp.zeros((b, 1, POOL_W), jnp.float32), jnp.cumsum(uf, axis=1)], axis=1)
    end = n_hist + jnp.arange(n_new) + 1
    pos = pos0 + jnp.arange(n_new)
    c_end = jnp.take(csum, end, axis=1)
    means = []
    for g, w in enumerate(POOL_WINDOWS):
        sl = slice(g * POOL_CG, (g + 1) * POOL_CG)
        start = jnp.maximum(end - w, 0)
        win = c_end[..., sl] - jnp.take(csum[..., sl], start, axis=1)
        cnt = jnp.minimum(pos + 1, w).astype(jnp.float32)
        means.append(win / cnt[None, :, None])
    pooled = jnp.stack(means, axis=2) - uf[:, n_hist:].reshape(b, n_new, POOL_GROUPS, POOL_CG)
    out = jnp.einsum('blgc,gce->blge', pooled, w_pool.astype(jnp.float32)).reshape(b, n_new, D_MODEL)
    return (out * pool_scale.astype(jnp.float32)).astype(u_ext.dtype)


def _hier_moe(h, w_group, b_group, w_router, b_router, w_gate, w_up, w_down):
    shape = h.shape
    hf = h.reshape(-1, D_MODEL)
    gl = jnp.dot(hf, w_group, preferred_element_type=jnp.float32) + b_group.astype(jnp.float32)
    g_val, g_idx = lax.top_k(jax.nn.softmax(gl, axis=-1), 1)
    el = jnp.einsum('nd,gde->nge', hf, w_router, preferred_element_type=jnp.float32) + b_router.astype(jnp.float32)
    el_sel = jnp.take_along_axis(el, g_idx[:, :, None], axis=1)[:, 0]
    e_val, e_idx = lax.top_k(el_sel, TOP_K_IN_GROUP)
    e_w = jax.nn.softmax(e_val, axis=-1) * g_val
    e_id = g_idx * EXPERTS_PER_GROUP + e_idx
    comb = jnp.einsum('nk,nke->ne', e_w, jax.nn.one_hot(e_id, N_EXPERTS, dtype=jnp.float32)).astype(h.dtype)
    y = jnp.zeros_like(hf)
    for e in range(N_EXPERTS):
        act = jax.nn.silu(hf @ w_gate[e]) * (hf @ w_up[e])
        y = y + comb[:, e:e + 1] * (act @ w_down[e])
    return y.reshape(shape)


def _merge(x, ga, gb, a, b, w_out):
    merged = jax.nn.sigmoid(ga) * a + jax.nn.sigmoid(gb) * b
    return x + merged @ w_out


def setup_inputs(seed: int = 0) -> dict:
    key = jax.random.key(seed)
    ks = jax.random.split(key, 32)
    nrm = jax.random.normal
    f = jnp.float32
    n_pages = PAST_LEN // PAGE_SIZE
    n_used = DEC_BATCH * n_pages
    n_phys = n_used + n_used // 4
    page_table = jax.random.permutation(ks[5], n_phys)[:n_used].reshape(DEC_BATCH, n_pages).astype(jnp.int32)
    return {
        'x_prompt': nrm(ks[0], (BATCH, SEQ, D_MODEL), f),
        'x_sample': nrm(ks[1], (DEC_BATCH, DEC_SEQ, D_MODEL), f),
        'cache_k': nrm(ks[2], (DEPTH, n_phys, PAGE_SIZE, N_HEADS, QK_DIM), f),
        'cache_v': nrm(ks[3], (DEPTH, n_phys, PAGE_SIZE, N_HEADS, DV), f),
        'state_pool': nrm(ks[4], (DEPTH, DEC_BATCH, POOL_HIST, POOL_W), f),
        'page_table': page_table,
        'meta_tokens': nrm(ks[6], (N_META, D_MODEL), f),
        'norm_mix_g': 1.0 + 0.05 * nrm(ks[7], (DEPTH, D_MODEL), f),
        'w_in': nrm(ks[8], (DEPTH, D_MODEL, IN_COLS), f) * D_MODEL ** -0.5,
        'lambda_q1': 0.1 * nrm(ks[9], (DEPTH, DK), f),
        'lambda_k1': 0.1 * nrm(ks[10], (DEPTH, DK), f),
        'lambda_q2': 0.1 * nrm(ks[11], (DEPTH, DK), f),
        'lambda_k2': 0.1 * nrm(ks[12], (DEPTH, DK), f),
        'subln_g': 1.0 + 0.05 * nrm(ks[13], (DEPTH, DV), f),
        'w_pool': nrm(ks[14], (DEPTH, POOL_GROUPS, POOL_CG, POOL_OUT_G), f) * POOL_CG ** -0.5,
        'pool_scale': 1.0 + 0.1 * nrm(ks[15], (DEPTH, D_MODEL), f),
        'w_out': nrm(ks[16], (DEPTH, D_MODEL, D_MODEL), f) * D_MODEL ** -0.5,
        'norm_ffn_g': 1.0 + 0.05 * nrm(ks[17], (DEPTH, D_MODEL), f),
        'w_group': nrm(ks[18], (DEPTH, D_MODEL, N_GROUPS), f) * D_MODEL ** -0.5,
        'b_group': 0.01 * nrm(ks[19], (DEPTH, N_GROUPS), f),
        'w_router': nrm(ks[20], (DEPTH, N_GROUPS, D_MODEL, EXPERTS_PER_GROUP), f) * D_MODEL ** -0.5,
        'b_router': 0.01 * nrm(ks[21], (DEPTH, N_GROUPS, EXPERTS_PER_GROUP), f),
        'w_gate': nrm(ks[22], (DEPTH, N_EXPERTS, D_MODEL, D_EXPERT), f) * D_MODEL ** -0.5,
        'w_up': nrm(ks[23], (DEPTH, N_EXPERTS, D_MODEL, D_EXPERT), f) * D_MODEL ** -0.5,
        'w_down': nrm(ks[24], (DEPTH, N_EXPERTS, D_EXPERT, D_MODEL), f) * D_EXPERT ** -0.5,
        'norm_final_g': 1.0 + 0.05 * nrm(ks[25], (D_MODEL,), f),
    }


def reference(x_prompt, x_sample, cache_k, cache_v, state_pool, page_table, meta_tokens,
              norm_mix_g, w_in, lambda_q1, lambda_k1, lambda_q2, lambda_k2, subln_g,
              w_pool, pool_scale, w_out, norm_ffn_g, w_group, b_group, w_router, b_router,
              w_gate, w_up, w_down, norm_final_g):
    b = x_prompt.shape[0]
    meta = jnp.broadcast_to(meta_tokens[None].astype(x_prompt.dtype), (b, N_META, D_MODEL))
    xp = jnp.concatenate([meta, x_prompt], axis=1)
    xs = x_sample
    nk_p, nv_p, npool_p, nk_s, nv_s, npool_s = [], [], [], [], [], []
    for l in range(DEPTH):
        lam_init = 0.8 - 0.6 * math.exp(-0.3 * l)
        lam = _diff_lambda(lambda_q1[l], lambda_k1[l], lambda_q2[l], lambda_k2[l], lam_init)
        qp, kp, vp, up, gap, gbp = _split_proj(_rmsnorm(xp, norm_mix_g[l]) @ w_in[l])
        ap = _head_out(_prompt_diff_attn(qp, kp, vp, lam), subln_g[l], lam_init, xp.dtype)
        bp = _pool_mix(up, 0, 0, w_pool[l], pool_scale[l])
        xp = _merge(xp, gap, gbp, ap, bp, w_out[l])
        xp = xp + _hier_moe(_rmsnorm(xp, norm_ffn_g[l]), w_group[l], b_group[l], w_router[l],
                            b_router[l], w_gate[l], w_up[l], w_down[l])
        qs, ks_, vs, us, gas, gbs = _split_proj(_rmsnorm(xs, norm_mix_g[l]) @ w_in[l])
        a_s = _head_out(_sample_diff_attn(qs, ks_, vs, cache_k[l], cache_v[l], page_table, lam),
                        subln_g[l], lam_init, xs.dtype)
        u_ext = jnp.concatenate([state_pool[l].astype(us.dtype), us], axis=1)
        b_s = _pool_mix(u_ext, POOL_HIST, PAST_LEN, w_pool[l], pool_scale[l])
        xs = _merge(xs, gas, gbs, a_s, b_s, w_out[l])
        xs = xs + _hier_moe(_rmsnorm(xs, norm_ffn_g[l]), w_group[l], b_group[l], w_router[l],
                            b_router[l], w_gate[l], w_up[l], w_down[l])
        nk_p.append(kp); nv_p.append(vp); npool_p.append(up[:, -POOL_HIST:])
        nk_s.append(ks_); nv_s.append(vs); npool_s.append(u_ext[:, -POOL_HIST:])
    y_prompt = _rmsnorm(xp, norm_final_g)[:, N_META:]
    y_sample = _rmsnorm(xs, norm_final_g)
    return (y_prompt, y_sample, jnp.stack(nk_p), jnp.stack(nv_p), jnp.stack(npool_p),
            jnp.stack(nk_s), jnp.stack(nv_s), jnp.stack(npool_s))
```

```python
import functools
import math

import jax
import jax.numpy as jnp
from jax import lax
from jax.experimental import pallas as pl
from jax.experimental.pallas import tpu as pltpu

F32 = jnp.float32
BF16 = jnp.bfloat16

DK = 64
HEAD = 2 * DK
N_META = 16
POOL_WINDOWS = (2, 4, 8, 16)
POOL_HIST = max(POOL_WINDOWS) - 1
HALO = 16
N_GROUPS = 4
EXPERTS_PER_GROUP = 8
N_PAIRS = EXPERTS_PER_GROUP * (EXPERTS_PER_GROUP - 1) // 2
N_CLASSES = N_GROUPS * N_PAIRS
EPS = 1e-6
ATTN_SCALE = DK ** -0.5
LANES = 128
NEG_INF = float("-inf")
VMEM_LIMIT = 56 * 1024 * 1024

TM_PROJ = 512
TQ = 256
TK = 256
PAGES_PER_STEP = 8
TM_MOE = 128


def _nt_dot(a, b):
    return lax.dot_general(a, b, (((1,), (1,)), ((), ())), preferred_element_type=F32)


def _dot(a, b):
    return jnp.dot(a, b, preferred_element_type=F32)


def _rms(x, g):
    r = lax.rsqrt(jnp.mean(x * x, axis=-1, keepdims=True) + EPS)
    return x * r * g


def _lam(lq1_ref, lk1_ref, lq2_ref, lk2_ref, lam_init):
    a = jnp.sum(lq1_ref[...] * lk1_ref[...], axis=-1, keepdims=True)
    b = jnp.sum(lq2_ref[...] * lk2_ref[...], axis=-1, keepdims=True)
    return jnp.exp(a) - jnp.exp(b) + lam_init


def _head_out(d, sg, lam_init):
    r = lax.rsqrt(jnp.mean(d * d, axis=-1, keepdims=True) + EPS)
    return d * r * sg * (1.0 - lam_init)


def _sigmoid(x):
    return 1.0 / (1.0 + jnp.exp(-x))


def _inproj_kernel(x_ref, g_ref, w_ref, *out_refs, dm, pw, n_heads, mode):
    h = _rms(x_ref[...], g_ref[...]).astype(BF16)

    def proj(c0, n):
        return _dot(h, w_ref[:, c0:c0 + n])

    def store_heads(ref, val):
        for hd in range(n_heads):
            ref[hd] = val[:, hd * HEAD:(hd + 1) * HEAD]

    if mode == "prompt":
        k_ref, v_ref, u_ref, sga_ref, sgb_ref, q_ref, kb_ref, vb_ref = out_refs
    elif mode == "sample":
        k_ref, v_ref, u_ref, sga_ref, sgb_ref, q_ref = out_refs
    else:
        k_ref, v_ref, u_ref, kb_ref, vb_ref = out_refs

    if mode == "prompt":
        store_heads(q_ref, (proj(0, dm) * ATTN_SCALE).astype(BF16))
    elif mode == "sample":
        q_ref[...] = proj(0, dm) * ATTN_SCALE
    k = proj(dm, dm)
    k_ref[...] = k
    v = proj(2 * dm, dm)
    v_ref[...] = v
    if mode != "sample":
        store_heads(kb_ref, k.astype(BF16))
        store_heads(vb_ref, v.astype(BF16))
    u_ref[...] = proj(3 * dm, pw)
    if mode != "meta":
        sga_ref[...] = _sigmoid(proj(3 * dm + pw, dm)).astype(BF16)
        sgb_ref[...] = _sigmoid(proj(4 * dm + pw, dm)).astype(BF16)


def _inproj(x, g, w_bf, *, mode, tm):
    n, dm = x.shape
    in_cols = w_bf.shape[1]
    pw = in_cols - 5 * dm
    n_heads = dm // HEAD
    assert n % tm == 0
    row = lambda c: pl.BlockSpec((tm, c), lambda i: (i, 0))
    heads = pl.BlockSpec((n_heads, tm, HEAD), lambda i: (0, i, 0))
    f_full = jax.ShapeDtypeStruct((n, dm), F32)
    b_full = jax.ShapeDtypeStruct((n, dm), BF16)
    b_heads = jax.ShapeDtypeStruct((n_heads, n, HEAD), BF16)
    out_shape = [f_full, f_full, jax.ShapeDtypeStruct((n, pw), F32)]
    out_specs = [row(dm), row(dm), row(pw)]
    if mode == "prompt":
        out_shape += [b_full, b_full, b_heads, b_heads, b_heads]
        out_specs += [row(dm), row(dm), heads, heads, heads]
    elif mode == "sample":
        out_shape += [b_full, b_full, f_full]
        out_specs += [row(dm), row(dm), row(dm)]
    else:
        out_shape += [b_heads, b_heads]
        out_specs += [heads, heads]
    return pl.pallas_call(
        functools.partial(_inproj_kernel, dm=dm, pw=pw, n_heads=n_heads, mode=mode),
        grid=(n // tm,),
        in_specs=[row(dm),
                  pl.BlockSpec((1, dm), lambda i: (0, 0)),
                  pl.BlockSpec((dm, in_cols), lambda i: (0, 0), pipeline_mode=pl.Buffered(1))],
        out_specs=out_specs,
        out_shape=out_shape,
        compiler_params=pltpu.CompilerParams(dimension_semantics=("arbitrary",), vmem_limit_bytes=VMEM_LIMIT),
        name="inproj_" + mode,
    )(x, g, w_bf)


def _pattn_kernel(q_ref, k_ref, v_ref, km_ref, vm_ref, lq1_ref, lk1_ref, lq2_ref, lk2_ref, sg_ref, o_ref,
                  m_s, l_s, acc_s, *, lam_init):
    i = pl.program_id(2)
    q = q_ref[...]
    lane = lax.broadcasted_iota(jnp.int32, q.shape, 1)
    zero = jnp.zeros_like(q)
    q2x = jnp.concatenate([jnp.where(lane < DK, q, zero), jnp.where(lane >= DK, q, zero)], axis=0)

    s = _nt_dot(q2x, km_ref[...])
    col = lax.broadcasted_iota(jnp.int32, s.shape, 1)
    s = jnp.where(col < N_META, s, NEG_INF)
    m0 = jnp.max(s, axis=-1, keepdims=True)
    p = jnp.exp(s - m0)
    m_s[...] = m0
    l_s[...] = jnp.sum(p, axis=-1, keepdims=True)
    acc_s[...] = _dot(p.astype(BF16), vm_ref[...])

    def update(s, vc):
        m_prev = m_s[...]
        m_new = jnp.maximum(m_prev, jnp.max(s, axis=-1, keepdims=True))
        corr = jnp.exp(m_prev - m_new)
        p = jnp.exp(s - m_new)
        l_s[...] = l_s[...] * corr + jnp.sum(p, axis=-1, keepdims=True)
        acc_s[...] = acc_s[...] * corr + _dot(p.astype(BF16), vc)
        m_s[...] = m_new

    def full_chunk(j, carry):
        start = pl.multiple_of(j * TK, TK)
        update(_nt_dot(q2x, k_ref[pl.ds(start, TK), :]), v_ref[pl.ds(start, TK), :])
        return carry

    lax.fori_loop(0, i, full_chunk, 0)

    start = pl.multiple_of(i * TK, TK)
    s = _nt_dot(q2x, k_ref[pl.ds(start, TK), :])
    r_in = lax.broadcasted_iota(jnp.int32, (TQ, TK), 0)
    c_in = lax.broadcasted_iota(jnp.int32, (TQ, TK), 1)
    keep = c_in <= r_in
    keep = jnp.concatenate([keep, keep], axis=0)
    update(jnp.where(keep, s, NEG_INF), v_ref[pl.ds(start, TK), :])

    lam = _lam(lq1_ref, lk1_ref, lq2_ref, lk2_ref, lam_init)
    o = acc_s[...] / l_s[...]
    d = o[:TQ] - lam * o[TQ:]
    o_ref[...] = _head_out(d, sg_ref[...], lam_init).astype(o_ref.dtype)


def _prompt_attention(q_hm, kb_hm, vb_hm, kbm_hm, vbm_hm, lam_vecs, sg, *, batch, seq, lam_init):
    n_heads, n, _ = q_hm.shape
    assert seq % TQ == 0 and TQ == TK
    nq = seq // TQ
    vec = lambda c: pl.BlockSpec((1, c), lambda b, h, i: (0, 0))
    kv = pl.BlockSpec((None, seq, HEAD), lambda b, h, i: (h, b, 0))
    meta = pl.BlockSpec((None, LANES, HEAD), lambda b, h, i: (h, 0, 0))
    qo = pl.BlockSpec((None, TQ, HEAD), lambda b, h, i: (h, b * nq + i, 0))
    return pl.pallas_call(
        functools.partial(_pattn_kernel, lam_init=lam_init),
        grid=(batch, n_heads, nq),
        in_specs=[qo, kv, kv, meta, meta, vec(DK), vec(DK), vec(DK), vec(DK), vec(HEAD)],
        out_specs=qo,
        out_shape=jax.ShapeDtypeStruct((n_heads, n, HEAD), BF16),
        scratch_shapes=[pltpu.VMEM((2 * TQ, 1), F32), pltpu.VMEM((2 * TQ, 1), F32),
                        pltpu.VMEM((2 * TQ, HEAD), F32)],
        compiler_params=pltpu.CompilerParams(dimension_semantics=("arbitrary",) * 3, vmem_limit_bytes=VMEM_LIMIT),
        name="prompt_attn",
    )(q_hm, kb_hm, vb_hm, kbm_hm, vbm_hm, *lam_vecs, sg)


def _sattn_kernel(pt_ref, ck_hbm, cv_hbm, q_ref, kn_ref, vn_ref, lq1_ref, lk1_ref, lq2_ref, lk2_ref, sg_ref, o_ref,
                  kbuf, vbuf, sem, wq_s, kb_s, vb_s, m_s, l_s, acc_s, *, pps, page, n_heads, dec_seq, lam_init):
    b = pl.program_id(0)
    j = pl.program_id(1)
    n_j = pl.num_programs(1)
    step = b * n_j + j
    slot = step % 2
    dm = n_heads * HEAD
    rows = n_heads * 2 * dec_seq

    def page_copies(bb, jj, sl):
        copies = []
        for p in range(pps):
            pg = pt_ref[bb, jj * pps + p]
            copies.append(pltpu.make_async_copy(ck_hbm.at[pg], kbuf.at[sl, p], sem.at[sl]))
            copies.append(pltpu.make_async_copy(cv_hbm.at[pg], vbuf.at[sl, p], sem.at[sl]))
        return copies

    @pl.when(step == 0)
    def _():
        for c in page_copies(b, j, slot):
            c.start()

    @pl.when(step + 1 < pl.num_programs(0) * n_j)
    def _():
        wrap = j + 1 == n_j
        for c in page_copies(jnp.where(wrap, b + 1, b), jnp.where(wrap, 0, j + 1), 1 - slot):
            c.start()

    for c in page_copies(b, j, slot):
        c.wait()

    @pl.when(j == 0)
    def _():
        qt = jnp.tile(q_ref[0], (rows // dec_seq, 1))
        r = lax.broadcasted_iota(jnp.int32, (rows, dm), 0)
        c = lax.broadcasted_iota(jnp.int32, (rows, dm), 1)
        own = (c // HEAD == r // (2 * dec_seq)) & ((c // DK) % 2 == (r // dec_seq) % 2)
        wq_s[...] = jnp.where(own, qt, 0.0).astype(BF16)
        m_s[...] = jnp.full(m_s.shape, NEG_INF, F32)
        l_s[...] = jnp.zeros(l_s.shape, F32)
        acc_s[...] = jnp.zeros(acc_s.shape, F32)

    def update(s, n_keys):
        m_prev = m_s[...]
        m_new = jnp.maximum(m_prev, jnp.max(s, axis=-1, keepdims=True))
        corr = jnp.exp(m_prev - m_new)
        p = jnp.exp(s - m_new)
        l_s[...] = l_s[...] * corr + jnp.sum(p, axis=-1, keepdims=True)
        pb = p.astype(BF16)
        pv = [_dot(pb[h * 2 * dec_seq:(h + 1) * 2 * dec_seq, :], vb_s[0:n_keys, h * HEAD:(h + 1) * HEAD])
              for h in range(n_heads)]
        acc_s[...] = acc_s[...] * corr + jnp.concatenate(pv, axis=0)
        m_s[...] = m_new

    for p in range(pps):
        for h in range(n_heads):
            rows_h = pl.ds(h, page, stride=n_heads)
            kb_s[p * page:(p + 1) * page, h * HEAD:(h + 1) * HEAD] = kbuf[slot, p, rows_h, :].astype(BF16)
            vb_s[p * page:(p + 1) * page, h * HEAD:(h + 1) * HEAD] = vbuf[slot, p, rows_h, :].astype(BF16)
    update(_nt_dot(wq_s[...], kb_s[...]), pps * page)

    @pl.when(j == pl.num_programs(1) - 1)
    def _():
        pad = jnp.zeros((page - dec_seq, dm), F32)
        kb_s[0:page, :] = jnp.concatenate([kn_ref[0], pad], axis=0).astype(BF16)
        vb_s[0:page, :] = jnp.concatenate([vn_ref[0], pad], axis=0).astype(BF16)
        s = _nt_dot(wq_s[...], kb_s[0:page, :])
        r = lax.broadcasted_iota(jnp.int32, s.shape, 0)
        c = lax.broadcasted_iota(jnp.int32, s.shape, 1)
        update(jnp.where(c <= r % dec_seq, s, NEG_INF), page)

        lam = _lam(lq1_ref, lk1_ref, lq2_ref, lk2_ref, lam_init)
        o = acc_s[...] / l_s[...]
        outs = []
        for h in range(n_heads):
            base = h * 2 * dec_seq
            d = o[base:base + dec_seq] - lam * o[base + dec_seq:base + 2 * dec_seq]
            outs.append(_head_out(d, sg_ref[...], lam_init))
        o_ref[0] = jnp.concatenate(outs, axis=-1)


def _sample_attention(page_table, cache_k, cache_v, q_s, k_s, v_s, lam_vecs, sg, *, lam_init):
    _, n_phys, page, n_heads, _ = cache_k.shape
    dm = n_heads * HEAD
    dec_batch, n_pages = page_table.shape
    dec_seq = q_s.shape[0] // dec_batch
    pps = PAGES_PER_STEP
    assert n_pages % pps == 0 and 2 * dec_seq * n_heads == LANES and dec_seq % 8 == 0
    ck = cache_k.reshape(n_phys, page * n_heads, HEAD)
    cv = cache_v.reshape(n_phys, page * n_heads, HEAD)
    q3, k3, v3 = (t.reshape(dec_batch, dec_seq, dm) for t in (q_s, k_s, v_s))

    tok = pl.BlockSpec((1, dec_seq, dm), lambda b, j, pt: (b, 0, 0))
    vec = lambda c: pl.BlockSpec((1, c), lambda b, j, pt: (0, 0))
    hbm = pl.BlockSpec(memory_space=pl.ANY)
    rows = n_heads * 2 * dec_seq
    grid_spec = pltpu.PrefetchScalarGridSpec(
        num_scalar_prefetch=1,
        grid=(dec_batch, n_pages // pps),
        in_specs=[hbm, hbm, tok, tok, tok, vec(DK), vec(DK), vec(DK), vec(DK), vec(HEAD)],
        out_specs=tok,
        scratch_shapes=[pltpu.VMEM((2, pps, page * n_heads, HEAD), F32), pltpu.VMEM((2, pps, page * n_heads, HEAD), F32),
                        pltpu.SemaphoreType.DMA((2,)),
                        pltpu.VMEM((rows, dm), BF16),
                        pltpu.VMEM((pps * page, dm), BF16), pltpu.VMEM((pps * page, dm), BF16),
                        pltpu.VMEM((rows, 1), F32), pltpu.VMEM((rows, 1), F32), pltpu.VMEM((rows, HEAD), F32)],
    )
    out = pl.pallas_call(
        functools.partial(_sattn_kernel, pps=pps, page=page, n_heads=n_heads, dec_seq=dec_seq, lam_init=lam_init),
        grid_spec=grid_spec,
        out_shape=jax.ShapeDtypeStruct((dec_batch, dec_seq, dm), F32),
        compiler_params=pltpu.CompilerParams(dimension_semantics=("arbitrary", "arbitrary"), vmem_limit_bytes=VMEM_LIMIT),
        name="sample_attn",
    )(page_table, ck, cv, q3, k3, v3, *lam_vecs, sg)
    return out.reshape(dec_batch * dec_seq, dm)


def _route(logits):
    lane = lax.broadcasted_iota(jnp.int32, logits.shape, 1)
    lane_f = lane.astype(F32)
    big = float(LANES)

    def first_max(vals):
        mx = jnp.max(vals, axis=-1, keepdims=True)
        idx = jnp.min(jnp.where(vals == mx, lane_f, big), axis=-1, keepdims=True)
        return mx, idx

    gl = jnp.where(lane < N_GROUPS, logits, NEG_INF)
    gmax, gidx = first_max(gl)
    g_val = 1.0 / jnp.sum(jnp.exp(gl - gmax), axis=-1, keepdims=True)
    lo = N_GROUPS + EXPERTS_PER_GROUP * gidx
    in_group = (lane_f >= lo) & (lane_f < lo + EXPERTS_PER_GROUP)
    el = jnp.where(in_group, logits, NEG_INF)
    v1, i1 = first_max(el)
    v2, i2 = first_max(jnp.where(lane_f == i1, NEG_INF, el))
    t = jnp.exp(v2 - v1)
    w1 = g_val / (1.0 + t)
    w2 = g_val * t / (1.0 + t)
    return i1 - N_GROUPS, i2 - N_GROUPS, w1, w2


def _merge_kernel(*refs, mode, tm, dm, pw, n_heads, tiles_per_seq):
    if mode == "prompt":
        (x_ref, a_ref, u_ref, uprev_ref, umeta_ref, sga_ref, sgb_ref, wp_ref, ps_ref, wo_ref, gf_ref,
         wrh_ref, wrl_ref, br_ref, x1_ref, route_ref) = refs
    else:
        (x_ref, a_ref, uext_ref, sga_ref, sgb_ref, wp_ref, ps_ref, wo_ref, gf_ref,
         wrh_ref, wrl_ref, br_ref, x1_ref, route_ref) = refs
    cg = pw // len(POOL_WINDOWS)

    if mode == "prompt":
        i = pl.program_id(0)
        a = jnp.concatenate([a_ref[h] for h in range(n_heads)], axis=-1).astype(F32)
        first = (i % tiles_per_seq) == 0
        halo = jnp.where(first, umeta_ref[...], uprev_ref[...])
        u = u_ref[...]
        run = jnp.concatenate([halo, u], axis=0)
        pooled = []
        width = 1
        for g, w in enumerate(POOL_WINDOWS):
            while width < w:
                run = run + pltpu.roll(run, shift=width, axis=0)
                width *= 2
            sl = slice(g * cg, (g + 1) * cg)
            pooled.append(run[HALO:, sl] / float(w) - u[:, sl])
    else:
        a = a_ref[...]
        seqs, ext_len, _ = uext_ref.shape
        n_new = ext_len - POOL_HIST
        pooled = []
        for g, w in enumerate(POOL_WINDOWS):
            sl = slice(g * cg, (g + 1) * cg)
            win = uext_ref[:, pl.ds(POOL_HIST, n_new), sl]
            for dlt in range(1, w):
                win = win + uext_ref[:, pl.ds(POOL_HIST - dlt, n_new), sl]
            tokv = uext_ref[:, pl.ds(POOL_HIST, n_new), sl]
            pooled.append((win / float(w) - tokv).reshape(seqs * n_new, cg))

    b = jnp.concatenate([_dot(pooled[g].astype(BF16), wp_ref[g]) for g in range(len(POOL_WINDOWS))], axis=-1)
    b = b * ps_ref[...]
    merged = sga_ref[...].astype(F32) * a + sgb_ref[...].astype(F32) * b
    x1 = x_ref[...] + _dot(merged.astype(BF16), wo_ref[...])
    x1_ref[...] = x1

    h = _rms(x1, gf_ref[...])
    h_hi = h.astype(BF16)
    h_lo = (h - h_hi.astype(F32)).astype(BF16)
    logits = (_dot(h_hi, wrh_ref[...]) + _dot(h_lo, wrh_ref[...]) + _dot(h_hi, wrl_ref[...])) + br_ref[...]
    e1, e2, w1, w2 = _route(logits)
    lane = lax.broadcasted_iota(jnp.int32, logits.shape, 1)
    route_ref[...] = jnp.where(lane == 0, e1, jnp.where(lane == 1, e2, jnp.where(lane == 2, w1, jnp.where(lane == 3, w2, 0.0))))


def _merge(x, a, u_parts, sga, sgb, wp_bf, ps, wo_bf, gf, wr_hi, wr_lo, br, *, mode, tm, seq):
    n, dm = x.shape
    pw = wp_bf.shape[0] * wp_bf.shape[1]
    n_heads = dm // HEAD
    assert n % tm == 0
    row = lambda c: pl.BlockSpec((tm, c), lambda i: (i, 0))
    const = lambda shape: pl.BlockSpec(shape, lambda i: (0,) * len(shape))
    if mode == "prompt":
        assert seq % tm == 0 and tm % HALO == 0
        u, u_meta = u_parts
        hb = tm // HALO
        a_spec = pl.BlockSpec((n_heads, tm, HEAD), lambda i: (0, i, 0))
        u_specs = [row(pw), pl.BlockSpec((HALO, pw), lambda i: (jnp.maximum(i * hb - 1, 0), 0)), const((HALO, pw))]
        u_args = [u, u, u_meta]
        tiles_per_seq = seq // tm
    else:
        (u_ext,) = u_parts
        n_new = u_ext.shape[1] - POOL_HIST
        assert tm % n_new == 0
        a_spec = row(dm)
        u_specs = [pl.BlockSpec((tm // n_new, u_ext.shape[1], pw), lambda i: (i, 0, 0))]
        u_args = [u_ext]
        tiles_per_seq = 1
    return pl.pallas_call(
        functools.partial(_merge_kernel, mode=mode, tm=tm, dm=dm, pw=pw, n_heads=n_heads, tiles_per_seq=tiles_per_seq),
        grid=(n // tm,),
        in_specs=[row(dm), a_spec, *u_specs, row(dm), row(dm),
                  const(wp_bf.shape), const((1, dm)), const((dm, dm)), const((1, dm)),
                  const((dm, LANES)), const((dm, LANES)), const((1, LANES))],
        out_specs=[row(dm), row(LANES)],
        out_shape=[jax.ShapeDtypeStruct((n, dm), F32), jax.ShapeDtypeStruct((n, LANES), F32)],
        compiler_params=pltpu.CompilerParams(dimension_semantics=("arbitrary",), vmem_limit_bytes=VMEM_LIMIT),
        name="merge_" + mode,
    )(x, a, *u_args, sga, sgb, wp_bf, ps, wo_bf, gf, wr_hi, wr_lo, br)


def _moe_kernel(ea_ref, eb_ref, nv_ref, tok_ref, tokn_ref, w_ref, xp_hbm, xs_hbm,
                wga_ref, wua_ref, wda_ref, wgb_ref, wub_ref, wdb_ref, gf_ref, gl_ref,
                op_hbm, os_hbm, xbuf, ybuf, gsem, ssem, *, tm, n_prompt):
    del ea_ref, eb_ref
    t = pl.program_id(0)
    n_t = pl.num_programs(0)
    slot = t % 2

    def gather_row(tok_smem, r, dst_slot):
        tok = tok_smem[0, 0, r]

        @pl.when(tok < n_prompt)
        def _():
            pltpu.make_async_copy(xp_hbm.at[pl.ds(tok, 1)], xbuf.at[dst_slot, pl.ds(r, 1)], gsem.at[dst_slot]).start()

        @pl.when(tok >= n_prompt)
        def _():
            pltpu.make_async_copy(xs_hbm.at[pl.ds(tok - n_prompt, 1)], xbuf.at[dst_slot, pl.ds(r, 1)], gsem.at[dst_slot]).start()

    def start_gather(tok_smem, dst_slot):
        def body(r, c):
            gather_row(tok_smem, r, dst_slot)
            return c
        lax.fori_loop(0, tm, body, 0)

    def wait_rows(sem, n_rows):
        def body(r, c):
            pltpu.make_async_copy(xp_hbm.at[pl.ds(0, 1)], xbuf.at[0, pl.ds(0, 1)], sem).wait()
            return c
        lax.fori_loop(0, n_rows, body, 0)

    @pl.when((t == 0) & (nv_ref[0] > 0))
    def _():
        start_gather(tok_ref, 0)

    @pl.when(t + 1 < n_t)
    def _():
        @pl.when(nv_ref[jnp.minimum(t + 1, n_t - 1)] > 0)
        def _():
            start_gather(tokn_ref, 1 - slot)

    @pl.when(t >= 2)
    def _():
        wait_rows(ssem.at[slot], nv_ref[jnp.maximum(t - 2, 0)])

    nv = nv_ref[t]

    @pl.when(nv > 0)
    def _():
        wait_rows(gsem.at[slot], tm)
        x = xbuf[slot]
        h = _rms(x, gf_ref[...]).astype(BF16)

        def expert(wg_ref, wu_ref, wd_ref):
            gate = _dot(h, wg_ref[...])
            up = _dot(h, wu_ref[...])
            act = gate * _sigmoid(gate) * up
            return _dot(act.astype(BF16), wd_ref[...])

        y = w_ref[:, 0:1] * expert(wga_ref, wua_ref, wda_ref)
        y = y + w_ref[:, 1:2] * expert(wgb_ref, wub_ref, wdb_ref)
        ybuf[slot] = _rms(x + y, gl_ref[...])

        def scatter(r, c):
            tok = tok_ref[0, 0, r]

            @pl.when(tok < n_prompt)
            def _():
                pltpu.make_async_copy(ybuf.at[slot, pl.ds(r, 1)], op_hbm.at[pl.ds(tok, 1)], ssem.at[slot]).start()

            @pl.when(tok >= n_prompt)
            def _():
                pltpu.make_async_copy(ybuf.at[slot, pl.ds(r, 1)], os_hbm.at[pl.ds(tok - n_prompt, 1)], ssem.at[slot]).start()
            return c
        lax.fori_loop(0, nv, scatter, 0)

    @pl.when(t == n_t - 1)
    def _():
        wait_rows(ssem.at[slot], nv)

        @pl.when(t >= 1)
        def _():
            wait_rows(ssem.at[1 - slot], nv_ref[jnp.maximum(t - 1, 0)])


def _moe(x1_p, x1_s, tile_ea, tile_eb, tile_nv, tok_tiles, w_tiles, wg_bf, wu_bf, wd_bf, gf, gl, *, tm):
    n_prompt, dm = x1_p.shape
    n_sample = x1_s.shape[0]
    n_tiles = tile_ea.shape[0]
    de = wg_bf.shape[2]
    tok_spec = lambda off: pl.BlockSpec((1, 1, tm), lambda t, ea, eb, nv: (jnp.minimum(t + off, n_tiles - 1), 0, 0),
                                        memory_space=pltpu.SMEM)
    wspec = lambda shape, which: pl.BlockSpec((None,) + shape, (lambda t, ea, eb, nv: (ea[t], 0, 0)) if which == 0
                                              else (lambda t, ea, eb, nv: (eb[t], 0, 0)))
    vec = pl.BlockSpec((1, dm), lambda t, ea, eb, nv: (0, 0))
    any_spec = pl.BlockSpec(memory_space=pl.ANY)
    grid_spec = pltpu.PrefetchScalarGridSpec(
        num_scalar_prefetch=3,
        grid=(n_tiles,),
        in_specs=[tok_spec(0), tok_spec(1),
                  pl.BlockSpec((tm, 2), lambda t, ea, eb, nv: (t, 0)),
                  any_spec, any_spec,
                  wspec((dm, de), 0), wspec((dm, de), 0), wspec((de, dm), 0),
                  wspec((dm, de), 1), wspec((dm, de), 1), wspec((de, dm), 1),
                  vec, vec],
        out_specs=[any_spec, any_spec],
        scratch_shapes=[pltpu.VMEM((2, tm, dm), F32), pltpu.VMEM((2, tm, dm), F32),
                        pltpu.SemaphoreType.DMA((2,)), pltpu.SemaphoreType.DMA((2,))],
    )
    return pl.pallas_call(
        functools.partial(_moe_kernel, tm=tm, n_prompt=n_prompt),
        grid_spec=grid_spec,
        out_shape=[jax.ShapeDtypeStruct((n_prompt, dm), F32), jax.ShapeDtypeStruct((n_sample, dm), F32)],
        compiler_params=pltpu.CompilerParams(dimension_semantics=("arbitrary",), vmem_limit_bytes=VMEM_LIMIT),
        name="moe",
    )(tile_ea, tile_eb, tile_nv, tok_tiles, tok_tiles, w_tiles, x1_p, x1_s,
      wg_bf, wu_bf, wd_bf, wg_bf, wu_bf, wd_bf, gf, gl)


def _moe_schedule(route, *, tm):
    n = route.shape[0]
    e1 = route[:, 0].astype(jnp.int32)
    e2 = route[:, 1].astype(jnp.int32)
    swap = e2 < e1
    ea = jnp.where(swap, e2, e1)
    eb = jnp.where(swap, e1, e2)
    wa = jnp.where(swap, route[:, 3], route[:, 2])
    wb = jnp.where(swap, route[:, 2], route[:, 3])
    grp = ea // EXPERTS_PER_GROUP
    la = ea % EXPERTS_PER_GROUP
    lb = eb % EXPERTS_PER_GROUP
    cls = grp * N_PAIRS + la * (2 * EXPERTS_PER_GROUP - 1 - la) // 2 + (lb - la - 1)
    cls_sorted, order = lax.sort_key_val(cls, jnp.arange(n, dtype=jnp.int32))
    row_end = jnp.searchsorted(cls_sorted, jnp.arange(N_CLASSES, dtype=jnp.int32), side="right").astype(jnp.int32)
    row_start = jnp.concatenate([jnp.zeros((1,), jnp.int32), row_end[:-1]])
    counts = row_end - row_start
    tiles_per = (counts + tm - 1) // tm
    tile_end = jnp.cumsum(tiles_per)
    tile_start = tile_end - tiles_per
    n_tiles = n // tm + N_CLASSES
    t = jnp.arange(n_tiles, dtype=jnp.int32)
    c_t = jnp.minimum(jnp.searchsorted(tile_end, t, side="right").astype(jnp.int32), N_CLASSES - 1)
    active = t < tile_end[-1]
    k_t = t - tile_start[c_t]
    left = jnp.where(active, counts[c_t] - k_t * tm, 0)
    nvalid = jnp.clip(left, 0, tm).astype(jnp.int32)
    last_c = c_t[jnp.maximum(tile_end[-1] - 1, 0)]
    c_eff = jnp.where(active, c_t, last_c)
    g_c = c_eff // N_PAIRS
    pair = c_eff % N_PAIRS
    la_tab, lb_tab = zip(*[(i, j) for i in range(EXPERTS_PER_GROUP) for j in range(i + 1, EXPERTS_PER_GROUP)])
    la_t = jnp.asarray(la_tab, jnp.int32)[pair]
    lb_t = jnp.asarray(lb_tab, jnp.int32)[pair]
    tile_ea = (g_c * EXPERTS_PER_GROUP + la_t).astype(jnp.int32)
    tile_eb = (g_c * EXPERTS_PER_GROUP + lb_t).astype(jnp.int32)
    r = jnp.arange(tm, dtype=jnp.int32)
    idx = row_start[c_t][:, None] + k_t[:, None] * tm + r[None, :]
    valid = r[None, :] < nvalid[:, None]
    tok = jnp.where(valid, order[jnp.clip(idx, 0, n - 1)], 0).astype(jnp.int32)
    w_tiles = jnp.stack([jnp.where(valid, wa[tok], 0.0), jnp.where(valid, wb[tok], 0.0)], axis=-1)
    return tile_ea, tile_eb, nvalid, tok.reshape(n_tiles, 1, tm), w_tiles.reshape(n_tiles * tm, 2)


def kernel(x_prompt, x_sample, cache_k, cache_v, state_pool, page_table, meta_tokens, norm_mix_g, w_in,
           lambda_q1, lambda_k1, lambda_q2, lambda_k2, subln_g, w_pool, pool_scale, w_out, norm_ffn_g,
           w_group, b_group, w_router, b_router, w_gate, w_up, w_down, norm_final_g):
    batch, seq, dm = x_prompt.shape
    dec_batch, dec_seq, _ = x_sample.shape
    depth = w_in.shape[0]
    assert depth == 1, "single-layer step"
    n_heads = dm // HEAD
    page = cache_k.shape[2]
    assert page_table.shape[1] * page >= POOL_HIST and N_META >= POOL_HIST + 1
    assert N_META <= HALO and w_group.shape[-1] == N_GROUPS and w_router.shape[-1] == EXPERTS_PER_GROUP
    lam_init = 0.8 - 0.6 * math.exp(-0.3 * 0)
    n_p, n_s = batch * seq, dec_batch * dec_seq
    pw = w_pool.shape[1] * w_pool.shape[2]

    w_in_bf = w_in[0].astype(BF16)
    g_mix = norm_mix_g[0].reshape(1, dm)
    lam_vecs = [v[0].reshape(1, DK) for v in (lambda_q1, lambda_k1, lambda_q2, lambda_k2)]
    sg = subln_g[0].reshape(1, HEAD)
    wp_bf = w_pool[0].astype(BF16)
    ps = pool_scale[0].reshape(1, dm)
    wo_bf = w_out[0].astype(BF16)
    g_ffn = norm_ffn_g[0].reshape(1, dm)
    g_fin = norm_final_g.reshape(1, dm)
    n_log = N_GROUPS + N_GROUPS * EXPERTS_PER_GROUP
    wr = jnp.concatenate([w_group[0], jnp.moveaxis(w_router[0], 0, 1).reshape(dm, -1)], axis=1)
    wr = jnp.pad(wr, ((0, 0), (0, LANES - n_log)))
    wr_hi = wr.astype(BF16)
    wr_lo = (wr - wr_hi.astype(F32)).astype(BF16)
    br = jnp.pad(jnp.concatenate([b_group[0], b_router[0].reshape(-1)]), (0, LANES - n_log)).reshape(1, LANES)
    wg_bf, wu_bf, wd_bf = w_gate[0].astype(BF16), w_up[0].astype(BF16), w_down[0].astype(BF16)

    xp = x_prompt.reshape(n_p, dm)
    xs = x_sample.reshape(n_s, dm)

    k_m, v_m, u_m, kb_m, vb_m = _inproj(meta_tokens, g_mix, w_in_bf, mode="meta", tm=N_META)
    k_p, v_p, u_p, sga_p, sgb_p, q_hm, kb_hm, vb_hm = _inproj(xp, g_mix, w_in_bf, mode="prompt", tm=TM_PROJ)
    k_s, v_s, u_s, sga_s, sgb_s, q_s = _inproj(xs, g_mix, w_in_bf, mode="sample", tm=TM_PROJ)

    pad_meta = lambda t: jnp.pad(t, ((0, 0), (0, LANES - N_META), (0, 0)))
    a_hm = _prompt_attention(q_hm, kb_hm, vb_hm, pad_meta(kb_m), pad_meta(vb_m), lam_vecs, sg,
                             batch=batch, seq=seq, lam_init=lam_init)
    a_s = _sample_attention(page_table, cache_k, cache_v, q_s, k_s, v_s, lam_vecs, sg, lam_init=lam_init)

    u_halo = jnp.pad(u_m, ((HALO - N_META, 0), (0, 0)))
    x1_p, route_p = _merge(xp, a_hm, (u_p, u_halo), sga_p, sgb_p, wp_bf, ps, wo_bf, g_ffn, wr_hi, wr_lo, br,
                           mode="prompt", tm=TM_PROJ, seq=seq)
    u_ext_s = jnp.concatenate([state_pool[0], u_s.reshape(dec_batch, dec_seq, pw)], axis=1)
    x1_s, route_s = _merge(xs, a_s, (u_ext_s,), sga_s, sgb_s, wp_bf, ps, wo_bf, g_ffn, wr_hi, wr_lo, br,
                           mode="sample", tm=TM_PROJ, seq=dec_seq)

    route = jnp.concatenate([route_p[:, :4], route_s[:, :4]], axis=0)
    tile_ea, tile_eb, tile_nv, tok_tiles, w_tiles = _moe_schedule(route, tm=TM_MOE)
    y_p, y_s = _moe(x1_p, x1_s, tile_ea, tile_eb, tile_nv, tok_tiles, w_tiles, wg_bf, wu_bf, wd_bf, g_ffn, g_fin,
                    tm=TM_MOE)

    t_ext = N_META + seq
    def with_meta(meta_rows, rows):
        m = jnp.broadcast_to(meta_rows.reshape(1, N_META, n_heads, HEAD), (batch, N_META, n_heads, HEAD))
        return jnp.concatenate([m, rows.reshape(batch, seq, n_heads, HEAD)], axis=1)[None]
    new_k_p = with_meta(k_m, k_p)
    new_v_p = with_meta(v_m, v_p)
    assert new_k_p.shape[2] == t_ext
    new_pool_p = u_p.reshape(batch, seq, pw)[:, seq - POOL_HIST:][None]
    new_k_s = k_s.reshape(1, dec_batch, dec_seq, n_heads, HEAD)
    new_v_s = v_s.reshape(1, dec_batch, dec_seq, n_heads, HEAD)
    new_pool_s = u_ext_s[:, u_ext_s.shape[1] - POOL_HIST:][None]
    return (y_p.reshape(batch, seq, dm), y_s.reshape(dec_batch, dec_seq, dm),
            new_k_p, new_v_p, new_pool_p, new_k_s, new_v_s, new_pool_s)
```

```python
import functools
import math

import jax
import jax.numpy as jnp
from jax import lax
from jax.experimental import pallas as pl
from jax.experimental.pallas import tpu as pltpu

F32 = jnp.float32
BF16 = jnp.bfloat16

DK = 64
HEAD = 2 * DK
N_META = 16
POOL_WINDOWS = (2, 4, 8, 16)
POOL_HIST = max(POOL_WINDOWS) - 1
HALO = 16
N_GROUPS = 4
EXPERTS_PER_GROUP = 8
N_PAIRS = EXPERTS_PER_GROUP * (EXPERTS_PER_GROUP - 1) // 2
N_CLASSES = N_GROUPS * N_PAIRS
EPS = 1e-6
LOG2E = 1.4426950408889634
Q_SCALE = DK ** -0.5 * LOG2E
LANES = 128
NEG_INF = float("-inf")
VMEM_LIMIT = 56 * 1024 * 1024

TM_PROJ = 512
TQ = 512
TK = 512
PAGES_PER_STEP = 8
PAGE_SLOTS = 3
TM_MOE = 128


def _nt_dot(a, b):
    return lax.dot_general(a, b, (((1,), (1,)), ((), ())), preferred_element_type=F32)


def _dot(a, b):
    return jnp.dot(a, b, preferred_element_type=F32)


def _rms(x, g):
    r = lax.rsqrt(jnp.mean(x * x, axis=-1, keepdims=True) + EPS)
    return x * r * g


def _lam(lq1_ref, lk1_ref, lq2_ref, lk2_ref, lam_init):
    a = jnp.sum(lq1_ref[...] * lk1_ref[...], axis=-1, keepdims=True)
    b = jnp.sum(lq2_ref[...] * lk2_ref[...], axis=-1, keepdims=True)
    return jnp.exp(a) - jnp.exp(b) + lam_init


def _head_out(d, sg, lam_init):
    r = lax.rsqrt(jnp.mean(d * d, axis=-1, keepdims=True) + EPS)
    return d * r * sg * (1.0 - lam_init)


def _sigmoid(x):
    return 1.0 / (1.0 + jnp.exp(-x))


def _lane_tiles(s):
    return [s[:, t * LANES:(t + 1) * LANES] for t in range(s.shape[1] // LANES)]


def _fold(op, parts):
    parts = list(parts)
    while len(parts) > 1:
        parts = [op(parts[a], parts[a + 1]) if a + 1 < len(parts) else parts[a] for a in range(0, len(parts), 2)]
    return parts[0]


def _inproj_kernel(x_ref, g_ref, w_ref, *out_refs, dm, pw, n_heads, mode):
    h = _rms(x_ref[...], g_ref[...]).astype(BF16)

    def proj(c0, n):
        return _dot(h, w_ref[:, c0:c0 + n])

    def store_heads(ref, val):
        for hd in range(n_heads):
            ref[hd] = val[:, hd * HEAD:(hd + 1) * HEAD]

    if mode == "prompt":
        k_ref, v_ref, u_ref, sga_ref, sgb_ref, q_ref, kb_ref, vb_ref = out_refs
    elif mode == "sample":
        k_ref, v_ref, u_ref, sga_ref, sgb_ref, q_ref = out_refs
    else:
        k_ref, v_ref, u_ref, kb_ref, vb_ref = out_refs

    if mode == "prompt":
        store_heads(q_ref, (proj(0, dm) * Q_SCALE).astype(BF16))
    elif mode == "sample":
        q_ref[...] = proj(0, dm) * Q_SCALE
    k = proj(dm, dm)
    k_ref[...] = k
    v = proj(2 * dm, dm)
    v_ref[...] = v
    if mode != "sample":
        store_heads(kb_ref, k.astype(BF16))
        store_heads(vb_ref, v.astype(BF16))
    u_ref[...] = proj(3 * dm, pw)
    if mode != "meta":
        sga_ref[...] = _sigmoid(proj(3 * dm + pw, dm)).astype(BF16)
        sgb_ref[...] = _sigmoid(proj(4 * dm + pw, dm)).astype(BF16)


def _inproj(x, g, w_bf, *, mode, tm):
    n, dm = x.shape
    in_cols = w_bf.shape[1]
    pw = in_cols - 5 * dm
    n_heads = dm // HEAD
    assert n % tm == 0
    row = lambda c: pl.BlockSpec((tm, c), lambda i: (i, 0))
    heads = pl.BlockSpec((n_heads, tm, HEAD), lambda i: (0, i, 0))
    f_full = jax.ShapeDtypeStruct((n, dm), F32)
    b_full = jax.ShapeDtypeStruct((n, dm), BF16)
    b_heads = jax.ShapeDtypeStruct((n_heads, n, HEAD), BF16)
    out_shape = [f_full, f_full, jax.ShapeDtypeStruct((n, pw), F32)]
    out_specs = [row(dm), row(dm), row(pw)]
    if mode == "prompt":
        out_shape += [b_full, b_full, b_heads, b_heads, b_heads]
        out_specs += [row(dm), row(dm), heads, heads, heads]
    elif mode == "sample":
        out_shape += [b_full, b_full, f_full]
        out_specs += [row(dm), row(dm), row(dm)]
    else:
        out_shape += [b_heads, b_heads]
        out_specs += [heads, heads]
    return pl.pallas_call(
        functools.partial(_inproj_kernel, dm=dm, pw=pw, n_heads=n_heads, mode=mode),
        grid=(n // tm,),
        in_specs=[row(dm),
                  pl.BlockSpec((1, dm), lambda i: (0, 0)),
                  pl.BlockSpec((dm, in_cols), lambda i: (0, 0), pipeline_mode=pl.Buffered(1))],
        out_specs=out_specs,
        out_shape=out_shape,
        compiler_params=pltpu.CompilerParams(dimension_semantics=("arbitrary",), vmem_limit_bytes=VMEM_LIMIT),
        name="inproj_" + mode,
    )(x, g, w_bf)


def _pattn_kernel(q_ref, k_ref, v_ref, km_ref, vm_ref, lq1_ref, lk1_ref, lq2_ref, lk2_ref, sg_ref, o_ref,
                  q2_s, mx_s, l_s, acc_s, *, lam_init):
    i = pl.program_id(2)
    q = q_ref[...]
    lane = lax.broadcasted_iota(jnp.int32, q.shape, 1)
    zero = jnp.zeros_like(q)
    q2_s[0:TQ, :] = jnp.where(lane < DK, q, zero)
    q2_s[TQ:2 * TQ, :] = jnp.where(lane >= DK, q, zero)

    def scores(j):
        start = pl.multiple_of(j * TK, TK)
        return _nt_dot(q2_s[...], k_ref[pl.ds(start, TK), :])

    def diag_scores():
        r_in = lax.broadcasted_iota(jnp.int32, (2 * TQ, TK), 0)
        c_in = lax.broadcasted_iota(jnp.int32, (2 * TQ, TK), 1)
        r_q = jnp.where(r_in >= TQ, r_in - TQ, r_in)
        return jnp.where(c_in <= r_q, scores(i), NEG_INF)

    def meta_scores():
        s = _nt_dot(q2_s[...], km_ref[...])
        col = lax.broadcasted_iota(jnp.int32, s.shape, 1)
        return jnp.where(col < N_META, s, NEG_INF)

    mx_s[...] = meta_scores()

    def max_chunk(j, carry):
        mx_s[...] = jnp.maximum(mx_s[...], _fold(jnp.maximum, _lane_tiles(scores(j))))
        return carry

    lax.fori_loop(0, i, max_chunk, 0)
    mx = jnp.maximum(mx_s[...], _fold(jnp.maximum, _lane_tiles(diag_scores())))
    mx_s[...] = jnp.broadcast_to(jnp.max(mx, axis=-1, keepdims=True), mx.shape)

    def accumulate(s, vc, first=False):
        m_b = mx_s[...]
        p_tiles = [jnp.exp2(t - m_b) for t in _lane_tiles(s)]
        psum = _fold(jnp.add, p_tiles)
        pv = _dot(jnp.concatenate([t.astype(BF16) for t in p_tiles], axis=-1), vc)
        if first:
            l_s[...] = psum
            acc_s[...] = pv
        else:
            l_s[...] += psum
            acc_s[...] += pv

    accumulate(meta_scores(), vm_ref[...], first=True)

    def pv_chunk(j, carry):
        start = pl.multiple_of(j * TK, TK)
        accumulate(scores(j), v_ref[pl.ds(start, TK), :])
        return carry

    lax.fori_loop(0, i, pv_chunk, 0)
    accumulate(diag_scores(), v_ref[pl.ds(pl.multiple_of(i * TK, TK), TK), :])

    lam = _lam(lq1_ref, lk1_ref, lq2_ref, lk2_ref, lam_init)
    o = acc_s[...] / jnp.sum(l_s[...], axis=-1, keepdims=True)
    d = o[:TQ] - lam * o[TQ:]
    o_ref[...] = _head_out(d, sg_ref[...], lam_init).astype(o_ref.dtype)


def _prompt_attention(q_hm, kb_hm, vb_hm, kbm_hm, vbm_hm, lam_vecs, sg, *, batch, seq, lam_init):
    n_heads, n, _ = q_hm.shape
    assert seq % TQ == 0 and TQ == TK
    nq = seq // TQ
    vec = lambda c: pl.BlockSpec((1, c), lambda b, h, i: (0, 0))
    kv = pl.BlockSpec((None, seq, HEAD), lambda b, h, i: (h, b, 0))
    meta = pl.BlockSpec((None, LANES, HEAD), lambda b, h, i: (h, 0, 0))
    qo = pl.BlockSpec((None, TQ, HEAD), lambda b, h, i: (h, b * nq + i, 0))
    return pl.pallas_call(
        functools.partial(_pattn_kernel, lam_init=lam_init),
        grid=(batch, n_heads, nq),
        in_specs=[qo, kv, kv, meta, meta, vec(DK), vec(DK), vec(DK), vec(DK), vec(HEAD)],
        out_specs=qo,
        out_shape=jax.ShapeDtypeStruct((n_heads, n, HEAD), BF16),
        scratch_shapes=[pltpu.VMEM((2 * TQ, HEAD), BF16), pltpu.VMEM((2 * TQ, LANES), F32),
                        pltpu.VMEM((2 * TQ, LANES), F32), pltpu.VMEM((2 * TQ, HEAD), F32)],
        compiler_params=pltpu.CompilerParams(dimension_semantics=("arbitrary",) * 3, vmem_limit_bytes=VMEM_LIMIT),
        name="prompt_attn",
    )(q_hm, kb_hm, vb_hm, kbm_hm, vbm_hm, *lam_vecs, sg)


def _sattn_kernel(pt_ref, ck_hbm, cv_hbm, q_ref, kn_ref, vn_ref, lq1_ref, lk1_ref, lq2_ref, lk2_ref, sg_ref, o_ref,
                  kbuf, vbuf, sem, wq_s, bias_s, kb_s, vb_s, s_s, m_s, l_s, acc_s,
                  *, pps, page, n_heads, dec_seq, lam_init, n_b, n_j):
    b = pl.program_id(0)
    j = pl.program_id(1)
    n_steps = n_b * n_j
    step = b * n_j + j
    slot = step % PAGE_SLOTS
    rows = n_heads * 2 * dec_seq
    pr = page * n_heads
    ahead = PAGE_SLOTS - 1

    def page_copies(s_idx, sl):
        bb = lax.div(s_idx, jnp.int32(n_j))
        jj = lax.rem(s_idx, jnp.int32(n_j))
        copies = []
        for p in range(pps):
            pg = pt_ref[bb, jj * pps + p]
            copies.append(pltpu.make_async_copy(ck_hbm.at[pg], kbuf.at[sl, p], sem.at[sl]))
            copies.append(pltpu.make_async_copy(cv_hbm.at[pg], vbuf.at[sl, p], sem.at[sl]))
        return copies

    @pl.when(step == 0)
    def _():
        for d in range(ahead):
            for c in page_copies(step + d, d):
                c.start()

    @pl.when(step + ahead < n_steps)
    def _():
        for c in page_copies(step + ahead, (step + ahead) % PAGE_SLOTS):
            c.start()

    for c in page_copies(step, slot):
        c.wait()

    @pl.when(j == 0)
    def _():
        q = q_ref[0]
        lane = lax.broadcasted_iota(jnp.int32, (dec_seq, HEAD), 1)
        blocks = []
        for h in range(n_heads):
            qh = q[:, h * HEAD:(h + 1) * HEAD]
            blocks += [jnp.where(lane < DK, qh, 0.0), jnp.where(lane >= DK, qh, 0.0)]
        wq_s[...] = jnp.concatenate(blocks, axis=0).astype(BF16)
        r = lax.broadcasted_iota(jnp.int32, (rows, LANES), 0)
        c = lax.broadcasted_iota(jnp.int32, (rows, LANES), 1)
        bias_s[...] = jnp.where(c % n_heads == r // (2 * dec_seq), 0.0, NEG_INF)
        m_s[...] = jnp.full(m_s.shape, NEG_INF, F32)
        l_s[...] = jnp.zeros(l_s.shape, F32)
        acc_s[...] = jnp.zeros(acc_s.shape, F32)

    def update(n_cols, extra_mask=None):
        bias = bias_s[...]
        tiles = []
        for t in range(n_cols // LANES):
            s_t = s_s[:, t * LANES:(t + 1) * LANES] + bias
            tiles.append(s_t if extra_mask is None else jnp.where(extra_mask, s_t, NEG_INF))
        m_prev = m_s[...]
        m_new = jnp.maximum(m_prev, jnp.max(_fold(jnp.maximum, tiles), axis=-1, keepdims=True))
        m_b = jnp.broadcast_to(m_new, (rows, LANES))
        corr = jnp.exp2(m_prev - m_new)
        p_tiles = [jnp.exp2(t - m_b) for t in tiles]
        l_s[...] = l_s[...] * corr + jnp.sum(_fold(jnp.add, p_tiles), axis=-1, keepdims=True)
        pb = jnp.concatenate([t.astype(BF16) for t in p_tiles], axis=-1)
        acc_s[...] = acc_s[...] * corr + _dot(pb, vb_s[0:n_cols, :])
        m_s[...] = m_new

    for p in range(pps):
        kb_s[p * pr:(p + 1) * pr, :] = kbuf[slot, p].astype(BF16)
        vb_s[p * pr:(p + 1) * pr, :] = vbuf[slot, p].astype(BF16)
        s_s[:, p * pr:(p + 1) * pr] = _nt_dot(wq_s[...], kb_s[p * pr:(p + 1) * pr, :])
    update(pps * pr)

    @pl.when(j == n_j - 1)
    def _():
        n_new = dec_seq * n_heads
        pad = jnp.zeros((LANES - n_new, HEAD), F32)
        kb_s[0:LANES, :] = jnp.concatenate([kn_ref[0], pad], axis=0).astype(BF16)
        vb_s[0:LANES, :] = jnp.concatenate([vn_ref[0], pad], axis=0).astype(BF16)
        s_s[:, 0:LANES] = _nt_dot(wq_s[...], kb_s[0:LANES, :])
        r = lax.broadcasted_iota(jnp.int32, (rows, LANES), 0)
        c = lax.broadcasted_iota(jnp.int32, (rows, LANES), 1)
        update(LANES, extra_mask=(c < n_new) & (c // n_heads <= r % dec_seq))

        lam = _lam(lq1_ref, lk1_ref, lq2_ref, lk2_ref, lam_init)
        o = acc_s[...] / l_s[...]
        outs = []
        for h in range(n_heads):
            base = h * 2 * dec_seq
            d = o[base:base + dec_seq] - lam * o[base + dec_seq:base + 2 * dec_seq]
            outs.append(_head_out(d, sg_ref[...], lam_init))
        o_ref[0] = jnp.concatenate(outs, axis=-1)


def _sample_attention(page_table, cache_k, cache_v, q_s, k_s, v_s, lam_vecs, sg, *, lam_init):
    _, n_phys, page, n_heads, _ = cache_k.shape
    dm = n_heads * HEAD
    dec_batch, n_pages = page_table.shape
    dec_seq = q_s.shape[0] // dec_batch
    pps = PAGES_PER_STEP
    pr = page * n_heads
    assert n_pages % pps == 0 and 2 * dec_seq * n_heads == LANES and dec_seq % 8 == 0
    assert dec_seq * n_heads <= LANES and LANES % n_heads == 0
    assert dec_batch * (n_pages // pps) >= PAGE_SLOTS
    ck = cache_k.reshape(n_phys, pr, HEAD)
    cv = cache_v.reshape(n_phys, pr, HEAD)
    q3 = q_s.reshape(dec_batch, dec_seq, dm)
    k3 = k_s.reshape(dec_batch, dec_seq * n_heads, HEAD)
    v3 = v_s.reshape(dec_batch, dec_seq * n_heads, HEAD)

    tok = pl.BlockSpec((1, dec_seq, dm), lambda b, j, pt: (b, 0, 0))
    new = pl.BlockSpec((1, dec_seq * n_heads, HEAD), lambda b, j, pt: (b, 0, 0))
    vec = lambda c: pl.BlockSpec((1, c), lambda b, j, pt: (0, 0))
    hbm = pl.BlockSpec(memory_space=pl.ANY)
    rows = n_heads * 2 * dec_seq
    grid_spec = pltpu.PrefetchScalarGridSpec(
        num_scalar_prefetch=1,
        grid=(dec_batch, n_pages // pps),
        in_specs=[hbm, hbm, tok, new, new, vec(DK), vec(DK), vec(DK), vec(DK), vec(HEAD)],
        out_specs=tok,
        scratch_shapes=[pltpu.VMEM((PAGE_SLOTS, pps, pr, HEAD), F32), pltpu.VMEM((PAGE_SLOTS, pps, pr, HEAD), F32),
                        pltpu.SemaphoreType.DMA((PAGE_SLOTS,)),
                        pltpu.VMEM((rows, HEAD), BF16), pltpu.VMEM((rows, LANES), F32),
                        pltpu.VMEM((pps * pr, HEAD), BF16), pltpu.VMEM((pps * pr, HEAD), BF16),
                        pltpu.VMEM((rows, pps * pr), F32),
                        pltpu.VMEM((rows, 1), F32), pltpu.VMEM((rows, 1), F32), pltpu.VMEM((rows, HEAD), F32)],
    )
    out = pl.pallas_call(
        functools.partial(_sattn_kernel, pps=pps, page=page, n_heads=n_heads, dec_seq=dec_seq, lam_init=lam_init,
                          n_b=dec_batch, n_j=n_pages // pps),
        grid_spec=grid_spec,
        out_shape=jax.ShapeDtypeStruct((dec_batch, dec_seq, dm), F32),
        compiler_params=pltpu.CompilerParams(dimension_semantics=("arbitrary", "arbitrary"), vmem_limit_bytes=VMEM_LIMIT),
        name="sample_attn",
    )(page_table, ck, cv, q3, k3, v3, *lam_vecs, sg)
    return out.reshape(dec_batch * dec_seq, dm)


def _route(logits):
    lane = lax.broadcasted_iota(jnp.int32, logits.shape, 1)
    lane_f = lane.astype(F32)
    big = float(LANES)

    def first_max(vals):
        mx = jnp.max(vals, axis=-1, keepdims=True)
        idx = jnp.min(jnp.where(vals == mx, lane_f, big), axis=-1, keepdims=True)
        return mx, idx

    gl = jnp.where(lane < N_GROUPS, logits, NEG_INF)
    gmax, gidx = first_max(gl)
    g_val = 1.0 / jnp.sum(jnp.exp(gl - gmax), axis=-1, keepdims=True)
    lo = N_GROUPS + EXPERTS_PER_GROUP * gidx
    in_group = (lane_f >= lo) & (lane_f < lo + EXPERTS_PER_GROUP)
    el = jnp.where(in_group, logits, NEG_INF)
    v1, i1 = first_max(el)
    v2, i2 = first_max(jnp.where(lane_f == i1, NEG_INF, el))
    t = jnp.exp(v2 - v1)
    w1 = g_val / (1.0 + t)
    w2 = g_val * t / (1.0 + t)
    return i1 - N_GROUPS, i2 - N_GROUPS, w1, w2


def _merge_kernel(*refs, n_own, **static):
    x1_ref = refs[-2]
    i = pl.program_id(0)

    @pl.when(i < n_own)
    def _():
        _merge_tile(i, *refs, **static)

    @pl.when(i >= n_own)
    def _():
        x1_ref[...] = jnp.zeros(x1_ref.shape, F32)


def _merge_tile(i, *refs, mode, tm, dm, pw, n_heads, tiles_per_seq):
    if mode == "prompt":
        (x_ref, a_ref, u_ref, uprev_ref, umeta_ref, sga_ref, sgb_ref, wp_ref, ps_ref, wo_ref, gf_ref,
         wrh_ref, wrl_ref, br_ref, x1_ref, route_ref) = refs
    else:
        (x_ref, a_ref, uext_ref, sga_ref, sgb_ref, wp_ref, ps_ref, wo_ref, gf_ref,
         wrh_ref, wrl_ref, br_ref, _, x1_ref, route_ref) = refs
    cg = pw // len(POOL_WINDOWS)

    if mode == "prompt":
        a = jnp.concatenate([a_ref[h] for h in range(n_heads)], axis=-1).astype(F32)
        first = (i % tiles_per_seq) == 0
        halo = jnp.where(first, umeta_ref[...], uprev_ref[...])
        u = u_ref[...]
        run = jnp.concatenate([halo, u], axis=0)
        pooled = []
        width = 1
        for g, w in enumerate(POOL_WINDOWS):
            while width < w:
                run = run + pltpu.roll(run, shift=width, axis=0)
                width *= 2
            sl = slice(g * cg, (g + 1) * cg)
            pooled.append(run[HALO:, sl] / float(w) - u[:, sl])
    else:
        a = a_ref[...]
        seqs, ext_len, _ = uext_ref.shape
        n_new = ext_len - POOL_HIST
        pooled = []
        for g, w in enumerate(POOL_WINDOWS):
            sl = slice(g * cg, (g + 1) * cg)
            win = uext_ref[:, pl.ds(POOL_HIST, n_new), sl]
            for dlt in range(1, w):
                win = win + uext_ref[:, pl.ds(POOL_HIST - dlt, n_new), sl]
            tokv = uext_ref[:, pl.ds(POOL_HIST, n_new), sl]
            pooled.append((win / float(w) - tokv).reshape(seqs * n_new, cg))

    b = jnp.concatenate([_dot(pooled[g].astype(BF16), wp_ref[g]) for g in range(len(POOL_WINDOWS))], axis=-1)
    b = b * ps_ref[...]
    merged = sga_ref[...].astype(F32) * a + sgb_ref[...].astype(F32) * b
    x1 = x_ref[...] + _dot(merged.astype(BF16), wo_ref[...])
    x1_ref[...] = x1

    h = _rms(x1, gf_ref[...])
    h_hi = h.astype(BF16)
    h_lo = (h - h_hi.astype(F32)).astype(BF16)
    logits = (_dot(h_hi, wrh_ref[...]) + _dot(h_lo, wrh_ref[...]) + _dot(h_hi, wrl_ref[...])) + br_ref[...]
    e1, e2, w1, w2 = _route(logits)
    lane = lax.broadcasted_iota(jnp.int32, logits.shape, 1)
    route_ref[...] = jnp.where(lane == 0, e1, jnp.where(lane == 1, e2, jnp.where(lane == 2, w1, jnp.where(lane == 3, w2, 0.0))))


def _merge(x, a, u_parts, sga, sgb, wp_bf, ps, wo_bf, gf, wr_hi, wr_lo, br, *, mode, tm, seq, n_total, x1_buf=None):
    n, dm = x.shape
    pw = wp_bf.shape[0] * wp_bf.shape[1]
    n_heads = dm // HEAD
    assert n % tm == 0 and n_total % tm == 0
    n_own = n // tm
    own = lambda i: jnp.minimum(i, n_own - 1)
    row = lambda c: pl.BlockSpec((tm, c), lambda i: (own(i), 0))
    const = lambda shape: pl.BlockSpec(shape, lambda i: (0,) * len(shape))
    extra_specs, extra_args, aliases = [], [], {}
    n_steps = n_own
    if mode == "prompt":
        assert seq % tm == 0 and tm % HALO == 0
        u, u_meta = u_parts
        hb = tm // HALO
        a_spec = pl.BlockSpec((n_heads, tm, HEAD), lambda i: (0, own(i), 0))
        u_specs = [row(pw), pl.BlockSpec((HALO, pw), lambda i: (jnp.maximum(own(i) * hb - 1, 0), 0)), const((HALO, pw))]
        u_args = [u, u, u_meta]
        tiles_per_seq = seq // tm
        x1_spec = pl.BlockSpec((tm, dm), lambda i: (i, 0))
        n_steps = n_total // tm
    else:
        (u_ext,) = u_parts
        n_new = u_ext.shape[1] - POOL_HIST
        assert tm % n_new == 0
        a_spec = row(dm)
        u_specs = [pl.BlockSpec((tm // n_new, u_ext.shape[1], pw), lambda i: (i, 0, 0))]
        u_args = [u_ext]
        tiles_per_seq = 1
        first_tile = (n_total - n) // tm
        x1_spec = pl.BlockSpec((tm, dm), lambda i: (i + first_tile, 0))
        extra_specs, extra_args = [pl.BlockSpec(memory_space=pl.ANY)], [x1_buf]
        aliases = {12 + len(u_specs) - 1: 0}
    in_specs = [row(dm), a_spec, *u_specs, row(dm), row(dm),
                const(wp_bf.shape), const((1, dm)), const((dm, dm)), const((1, dm)),
                const((dm, LANES)), const((dm, LANES)), const((1, LANES)), *extra_specs]
    args = [x, a, *u_args, sga, sgb, wp_bf, ps, wo_bf, gf, wr_hi, wr_lo, br, *extra_args]
    if aliases:
        assert args[next(iter(aliases))] is x1_buf
    return pl.pallas_call(
        functools.partial(_merge_kernel, n_own=n_own, mode=mode, tm=tm, dm=dm, pw=pw, n_heads=n_heads,
                          tiles_per_seq=tiles_per_seq),
        grid=(n_steps,),
        in_specs=in_specs,
        out_specs=[x1_spec, row(LANES)],
        out_shape=[jax.ShapeDtypeStruct((n_total, dm), F32), jax.ShapeDtypeStruct((n, LANES), F32)],
        input_output_aliases=aliases,
        compiler_params=pltpu.CompilerParams(dimension_semantics=("arbitrary",), vmem_limit_bytes=VMEM_LIMIT),
        name="merge_" + mode,
    )(*args)


def _moe_kernel(ea_ref, eb_ref, nv_ref, tok_ref, tokn_ref, w_ref, x_hbm,
                wga_ref, wua_ref, wda_ref, wgb_ref, wub_ref, wdb_ref, gf_ref, gl_ref,
                o_hbm, xbuf, ybuf, gsem, ssem, *, tm):
    del ea_ref, eb_ref
    t = pl.program_id(0)
    n_t = pl.num_programs(0)
    slot = t % 2

    def gather(tok_smem, r, sl):
        return pltpu.make_async_copy(x_hbm.at[pl.ds(tok_smem[0, 0, r], 1)], xbuf.at[sl, pl.ds(r, 1)], gsem.at[sl])

    def scatter(r, sl, dst):
        return pltpu.make_async_copy(ybuf.at[sl, pl.ds(r, 1)], o_hbm.at[pl.ds(dst, 1)], ssem.at[sl])

    def for_rows(n, fn):
        @pl.when(n == tm)
        def _():
            for r in range(tm):
                fn(r)

        @pl.when(n < tm)
        def _():
            def body(r, carry):
                fn(r)
                return carry
            lax.fori_loop(0, n, body, 0)

    def wait_scatters(sl, n):
        for_rows(n, lambda r: scatter(r, sl, 0).wait())

    @pl.when((t == 0) & (nv_ref[0] > 0))
    def _():
        for r in range(tm):
            gather(tok_ref, r, 0).start()

    @pl.when((t + 1 < n_t) & (nv_ref[jnp.minimum(t + 1, n_t - 1)] > 0))
    def _():
        for r in range(tm):
            gather(tokn_ref, r, 1 - slot).start()

    @pl.when(t >= 2)
    def _():
        wait_scatters(slot, nv_ref[jnp.maximum(t - 2, 0)])

    nv = nv_ref[t]

    @pl.when(nv > 0)
    def _():
        for r in range(tm):
            gather(tok_ref, r, slot).wait()
        x = xbuf[slot]
        h = _rms(x, gf_ref[...]).astype(BF16)

        def expert(wg_ref, wu_ref, wd_ref):
            gate = _dot(h, wg_ref[...])
            up = _dot(h, wu_ref[...])
            act = gate * _sigmoid(gate) * up
            return _dot(act.astype(BF16), wd_ref[...])

        y = w_ref[:, 0:1] * expert(wga_ref, wua_ref, wda_ref)
        y = y + w_ref[:, 1:2] * expert(wgb_ref, wub_ref, wdb_ref)
        ybuf[slot] = _rms(x + y, gl_ref[...])
        for_rows(nv, lambda r: scatter(r, slot, tok_ref[0, 0, r]).start())

    @pl.when(t == n_t - 1)
    def _():
        wait_scatters(slot, nv)

        @pl.when(t >= 1)
        def _():
            wait_scatters(1 - slot, nv_ref[jnp.maximum(t - 1, 0)])


def _moe(x1, tile_ea, tile_eb, tile_nv, tok_tiles, w_tiles, wg_bf, wu_bf, wd_bf, gf, gl, *, tm):
    n_tok, dm = x1.shape
    n_tiles = tile_ea.shape[0]
    de = wg_bf.shape[2]
    tok_spec = lambda off: pl.BlockSpec((1, 1, tm), lambda t, ea, eb, nv: (jnp.minimum(t + off, n_tiles - 1), 0, 0),
                                        memory_space=pltpu.SMEM)
    wspec = lambda shape, which: pl.BlockSpec((None,) + shape, (lambda t, ea, eb, nv: (ea[t], 0, 0)) if which == 0
                                              else (lambda t, ea, eb, nv: (eb[t], 0, 0)))
    vec = pl.BlockSpec((1, dm), lambda t, ea, eb, nv: (0, 0))
    any_spec = pl.BlockSpec(memory_space=pl.ANY)
    grid_spec = pltpu.PrefetchScalarGridSpec(
        num_scalar_prefetch=3,
        grid=(n_tiles,),
        in_specs=[tok_spec(0), tok_spec(1),
                  pl.BlockSpec((tm, 2), lambda t, ea, eb, nv: (t, 0)),
                  any_spec,
                  wspec((dm, de), 0), wspec((dm, de), 0), wspec((de, dm), 0),
                  wspec((dm, de), 1), wspec((dm, de), 1), wspec((de, dm), 1),
                  vec, vec],
        out_specs=any_spec,
        scratch_shapes=[pltpu.VMEM((2, tm, dm), F32), pltpu.VMEM((2, tm, dm), F32),
                        pltpu.SemaphoreType.DMA((2,)), pltpu.SemaphoreType.DMA((2,))],
    )
    return pl.pallas_call(
        functools.partial(_moe_kernel, tm=tm),
        grid_spec=grid_spec,
        out_shape=jax.ShapeDtypeStruct((n_tok, dm), F32),
        compiler_params=pltpu.CompilerParams(dimension_semantics=("arbitrary",), vmem_limit_bytes=VMEM_LIMIT),
        name="moe",
    )(tile_ea, tile_eb, tile_nv, tok_tiles, tok_tiles, w_tiles, x1,
      wg_bf, wu_bf, wd_bf, wg_bf, wu_bf, wd_bf, gf, gl)


def _moe_schedule(route, *, tm):
    n = route.shape[0]
    e1 = route[:, 0].astype(jnp.int32)
    e2 = route[:, 1].astype(jnp.int32)
    swap = e2 < e1
    ea = jnp.where(swap, e2, e1)
    eb = jnp.where(swap, e1, e2)
    wa = jnp.where(swap, route[:, 3], route[:, 2])
    wb = jnp.where(swap, route[:, 2], route[:, 3])
    grp = ea // EXPERTS_PER_GROUP
    la = ea % EXPERTS_PER_GROUP
    lb = eb % EXPERTS_PER_GROUP
    cls = grp * N_PAIRS + la * (2 * EXPERTS_PER_GROUP - 1 - la) // 2 + (lb - la - 1)
    cls_sorted, order = lax.sort_key_val(cls, jnp.arange(n, dtype=jnp.int32))
    row_end = jnp.searchsorted(cls_sorted, jnp.arange(N_CLASSES, dtype=jnp.int32), side="right").astype(jnp.int32)
    row_start = jnp.concatenate([jnp.zeros((1,), jnp.int32), row_end[:-1]])
    counts = row_end - row_start
    tiles_per = (counts + tm - 1) // tm
    tile_end = jnp.cumsum(tiles_per)
    tile_start = tile_end - tiles_per
    n_tiles = n // tm + N_CLASSES
    t = jnp.arange(n_tiles, dtype=jnp.int32)
    c_t = jnp.minimum(jnp.searchsorted(tile_end, t, side="right").astype(jnp.int32), N_CLASSES - 1)
    active = t < tile_end[-1]
    k_t = t - tile_start[c_t]
    left = jnp.where(active, counts[c_t] - k_t * tm, 0)
    nvalid = jnp.clip(left, 0, tm).astype(jnp.int32)
    last_c = c_t[jnp.maximum(tile_end[-1] - 1, 0)]
    c_eff = jnp.where(active, c_t, last_c)
    g_c = c_eff // N_PAIRS
    pair = c_eff % N_PAIRS
    la_tab, lb_tab = zip(*[(i, j) for i in range(EXPERTS_PER_GROUP) for j in range(i + 1, EXPERTS_PER_GROUP)])
    la_t = jnp.asarray(la_tab, jnp.int32)[pair]
    lb_t = jnp.asarray(lb_tab, jnp.int32)[pair]
    tile_ea = (g_c * EXPERTS_PER_GROUP + la_t).astype(jnp.int32)
    tile_eb = (g_c * EXPERTS_PER_GROUP + lb_t).astype(jnp.int32)
    r = jnp.arange(tm, dtype=jnp.int32)
    idx = row_start[c_t][:, None] + k_t[:, None] * tm + r[None, :]
    valid = r[None, :] < nvalid[:, None]
    tok = jnp.where(valid, order[jnp.clip(idx, 0, n - 1)], 0).astype(jnp.int32)
    w_tiles = jnp.stack([jnp.where(valid, wa[tok], 0.0), jnp.where(valid, wb[tok], 0.0)], axis=-1)
    return tile_ea, tile_eb, nvalid, tok.reshape(n_tiles, 1, tm), w_tiles.reshape(n_tiles * tm, 2)


def kernel(x_prompt, x_sample, cache_k, cache_v, state_pool, page_table, meta_tokens, norm_mix_g, w_in,
           lambda_q1, lambda_k1, lambda_q2, lambda_k2, subln_g, w_pool, pool_scale, w_out, norm_ffn_g,
           w_group, b_group, w_router, b_router, w_gate, w_up, w_down, norm_final_g):
    batch, seq, dm = x_prompt.shape
    dec_batch, dec_seq, _ = x_sample.shape
    depth = w_in.shape[0]
    assert depth == 1, "single-layer step"
    n_heads = dm // HEAD
    page = cache_k.shape[2]
    assert page_table.shape[1] * page >= POOL_HIST and N_META >= POOL_HIST + 1
    assert N_META <= HALO and w_group.shape[-1] == N_GROUPS and w_router.shape[-1] == EXPERTS_PER_GROUP
    lam_init = 0.8 - 0.6 * math.exp(-0.3 * 0)
    n_p, n_s = batch * seq, dec_batch * dec_seq
    n_tok = n_p + n_s
    pw = w_pool.shape[1] * w_pool.shape[2]

    w_in_bf = w_in[0].astype(BF16)
    g_mix = norm_mix_g[0].reshape(1, dm)
    lam_vecs = [v[0].reshape(1, DK) for v in (lambda_q1, lambda_k1, lambda_q2, lambda_k2)]
    sg = subln_g[0].reshape(1, HEAD)
    wp_bf = w_pool[0].astype(BF16)
    ps = pool_scale[0].reshape(1, dm)
    wo_bf = w_out[0].astype(BF16)
    g_ffn = norm_ffn_g[0].reshape(1, dm)
    g_fin = norm_final_g.reshape(1, dm)
    n_log = N_GROUPS + N_GROUPS * EXPERTS_PER_GROUP
    wr = jnp.concatenate([w_group[0], jnp.moveaxis(w_router[0], 0, 1).reshape(dm, -1)], axis=1)
    wr = jnp.pad(wr, ((0, 0), (0, LANES - n_log)))
    wr_hi = wr.astype(BF16)
    wr_lo = (wr - wr_hi.astype(F32)).astype(BF16)
    br = jnp.pad(jnp.concatenate([b_group[0], b_router[0].reshape(-1)]), (0, LANES - n_log)).reshape(1, LANES)
    wg_bf, wu_bf, wd_bf = w_gate[0].astype(BF16), w_up[0].astype(BF16), w_down[0].astype(BF16)

    xp = x_prompt.reshape(n_p, dm)
    xs = x_sample.reshape(n_s, dm)

    k_m, v_m, u_m, kb_m, vb_m = _inproj(meta_tokens, g_mix, w_in_bf, mode="meta", tm=N_META)
    k_p, v_p, u_p, sga_p, sgb_p, q_hm, kb_hm, vb_hm = _inproj(xp, g_mix, w_in_bf, mode="prompt", tm=TM_PROJ)
    k_s, v_s, u_s, sga_s, sgb_s, q_s = _inproj(xs, g_mix, w_in_bf, mode="sample", tm=TM_PROJ)

    pad_meta = lambda t: jnp.pad(t, ((0, 0), (0, LANES - N_META), (0, 0)))
    a_hm = _prompt_attention(q_hm, kb_hm, vb_hm, pad_meta(kb_m), pad_meta(vb_m), lam_vecs, sg,
                             batch=batch, seq=seq, lam_init=lam_init)
    a_s = _sample_attention(page_table, cache_k, cache_v, q_s, k_s, v_s, lam_vecs, sg, lam_init=lam_init)

    u_halo = jnp.pad(u_m, ((HALO - N_META, 0), (0, 0)))
    x1, route_p = _merge(xp, a_hm, (u_p, u_halo), sga_p, sgb_p, wp_bf, ps, wo_bf, g_ffn, wr_hi, wr_lo, br,
                         mode="prompt", tm=TM_PROJ, seq=seq, n_total=n_tok)
    u_ext_s = jnp.concatenate([state_pool[0], u_s.reshape(dec_batch, dec_seq, pw)], axis=1)
    x1, route_s = _merge(xs, a_s, (u_ext_s,), sga_s, sgb_s, wp_bf, ps, wo_bf, g_ffn, wr_hi, wr_lo, br,
                         mode="sample", tm=TM_PROJ, seq=dec_seq, n_total=n_tok, x1_buf=x1)

    route = jnp.concatenate([route_p[:, :4], route_s[:, :4]], axis=0)
    tile_ea, tile_eb, tile_nv, tok_tiles, w_tiles = _moe_schedule(route, tm=TM_MOE)
    y = _moe(x1, tile_ea, tile_eb, tile_nv, tok_tiles, w_tiles, wg_bf, wu_bf, wd_bf, g_ffn, g_fin, tm=TM_MOE)

    def with_meta(meta_rows, rows):
        m = jnp.broadcast_to(meta_rows.reshape(1, N_META, n_heads, HEAD), (batch, N_META, n_heads, HEAD))
        return jnp.concatenate([m, rows.reshape(batch, seq, n_heads, HEAD)], axis=1)[None]
    new_k_p = with_meta(k_m, k_p)
    new_v_p = with_meta(v_m, v_p)
    new_pool_p = u_p.reshape(batch, seq, pw)[:, seq - POOL_HIST:][None]
    new_k_s = k_s.reshape(1, dec_batch, dec_seq, n_heads, HEAD)
    new_v_s = v_s.reshape(1, dec_batch, dec_seq, n_heads, HEAD)
    new_pool_s = u_ext_s[:, u_ext_s.shape[1] - POOL_HIST:][None]
    return (y[:n_p].reshape(batch, seq, dm), y[n_p:n_tok].reshape(dec_batch, dec_seq, dm),
            new_k_p, new_v_p, new_pool_p, new_k_s, new_v_s, new_pool_s)
```

```python
import functools
import math

import jax
import jax.numpy as jnp
from jax import lax
from jax.experimental import pallas as pl
from jax.experimental.pallas import tpu as pltpu

F32 = jnp.float32
BF16 = jnp.bfloat16

DK = 64
HEAD = 2 * DK
N_META = 16
POOL_WINDOWS = (2, 4, 8, 16)
POOL_HIST = max(POOL_WINDOWS) - 1
HALO = 16
N_GROUPS = 4
EXPERTS_PER_GROUP = 8
N_PAIRS = EXPERTS_PER_GROUP * (EXPERTS_PER_GROUP - 1) // 2
N_CLASSES = N_GROUPS * N_PAIRS
EPS = 1e-6
LOG2E = 1.4426950408889634
Q_SCALE = DK ** -0.5 * LOG2E
LANES = 128
NEG_INF = float("-inf")
VMEM_LIMIT = 56 * 1024 * 1024

TM_PROJ = 512
TQ = 512
TK = 512
PAGES_PER_STEP = 8
PAGE_SLOTS = 3
TM_MOE = 128


def _nt_dot(a, b):
    return lax.dot_general(a, b, (((1,), (1,)), ((), ())), preferred_element_type=F32)


def _dot(a, b):
    return jnp.dot(a, b, preferred_element_type=F32)


def _rms(x, g):
    r = lax.rsqrt(jnp.mean(x * x, axis=-1, keepdims=True) + EPS)
    return x * r * g


def _lam(lq1_ref, lk1_ref, lq2_ref, lk2_ref, lam_init):
    a = jnp.sum(lq1_ref[...] * lk1_ref[...], axis=-1, keepdims=True)
    b = jnp.sum(lq2_ref[...] * lk2_ref[...], axis=-1, keepdims=True)
    return jnp.exp(a) - jnp.exp(b) + lam_init


def _head_out(d, sg, lam_init):
    r = lax.rsqrt(jnp.mean(d * d, axis=-1, keepdims=True) + EPS)
    return d * r * sg * (1.0 - lam_init)


def _sigmoid(x):
    return 1.0 / (1.0 + jnp.exp(-x))


def _lane_tiles(s):
    return [s[:, t * LANES:(t + 1) * LANES] for t in range(s.shape[1] // LANES)]


def _fold(op, parts):
    parts = list(parts)
    while len(parts) > 1:
        parts = [op(parts[a], parts[a + 1]) if a + 1 < len(parts) else parts[a] for a in range(0, len(parts), 2)]
    return parts[0]


def _inproj_kernel(*refs, dm, pw, n_heads, mode, tm, tiles_per_seq):
    if mode == "prompt":
        (x_ref, g_ref, w_ref, kmeta_ref, vmeta_ref, kout_hbm, vout_hbm, u_ref, sga_ref, sgb_ref, q_ref, kb_ref, vb_ref,
         kst, vst, sem, msem) = refs
    elif mode == "sample":
        x_ref, g_ref, w_ref, k_ref, v_ref, u_ref, sga_ref, sgb_ref, q_ref = refs
    else:
        x_ref, g_ref, w_ref, k_ref, v_ref, u_ref, kb_ref, vb_ref = refs

    h = _rms(x_ref[...], g_ref[...]).astype(BF16)

    def proj(c0, n):
        return _dot(h, w_ref[:, c0:c0 + n])

    def store_heads(ref, val):
        for hd in range(n_heads):
            ref[hd] = val[:, hd * HEAD:(hd + 1) * HEAD]

    if mode == "prompt":
        i = pl.program_id(0)
        n_i = pl.num_programs(0)
        slot = i % 2

        def tile_copies(step, sl):
            bb = lax.div(step, jnp.int32(tiles_per_seq))
            rows = pl.ds(N_META + lax.rem(step, jnp.int32(tiles_per_seq)) * tm, tm)
            return [pltpu.make_async_copy(kst.at[sl], kout_hbm.at[bb, rows], sem.at[sl]),
                    pltpu.make_async_copy(vst.at[sl], vout_hbm.at[bb, rows], sem.at[sl])]

        def meta_copies(step):
            bb = lax.div(step, jnp.int32(tiles_per_seq))
            return [pltpu.make_async_copy(kmeta_ref, kout_hbm.at[bb, pl.ds(0, N_META)], msem.at[0]),
                    pltpu.make_async_copy(vmeta_ref, vout_hbm.at[bb, pl.ds(0, N_META)], msem.at[0])]

        @pl.when(lax.rem(i, jnp.int32(tiles_per_seq)) == 0)
        def _():
            for c in meta_copies(i):
                c.start()

        @pl.when(i >= 2)
        def _():
            for c in tile_copies(i - 2, slot):
                c.wait()

    if mode == "prompt":
        store_heads(q_ref, (proj(0, dm) * Q_SCALE).astype(BF16))
    elif mode == "sample":
        q_ref[...] = proj(0, dm) * Q_SCALE
    k = proj(dm, dm)
    v = proj(2 * dm, dm)
    if mode == "prompt":
        kst[slot] = k
        vst[slot] = v
        for c in tile_copies(i, slot):
            c.start()
    else:
        k_ref[...] = k
        v_ref[...] = v
    if mode != "sample":
        store_heads(kb_ref, k.astype(BF16))
        store_heads(vb_ref, v.astype(BF16))
    u_ref[...] = proj(3 * dm, pw)
    if mode != "meta":
        sga_ref[...] = _sigmoid(proj(3 * dm + pw, dm)).astype(BF16)
        sgb_ref[...] = _sigmoid(proj(4 * dm + pw, dm)).astype(BF16)

    if mode == "prompt":
        @pl.when(lax.rem(i, jnp.int32(tiles_per_seq)) == 0)
        def _():
            for c in meta_copies(i):
                c.wait()

        @pl.when(i == n_i - 1)
        def _():
            for c in tile_copies(i, slot):
                c.wait()

            @pl.when(i >= 1)
            def _():
                for c in tile_copies(i - 1, 1 - slot):
                    c.wait()


def _inproj(x, g, w_bf, *, mode, tm, seq=None, meta_kv=None):
    n, dm = x.shape
    in_cols = w_bf.shape[1]
    pw = in_cols - 5 * dm
    n_heads = dm // HEAD
    assert n % tm == 0
    row = lambda c: pl.BlockSpec((tm, c), lambda i: (i, 0))
    const = lambda shape: pl.BlockSpec(shape, lambda i: (0,) * len(shape))
    heads = pl.BlockSpec((n_heads, tm, HEAD), lambda i: (0, i, 0))
    f_full = jax.ShapeDtypeStruct((n, dm), F32)
    b_full = jax.ShapeDtypeStruct((n, dm), BF16)
    b_heads = jax.ShapeDtypeStruct((n_heads, n, HEAD), BF16)
    u_shape = jax.ShapeDtypeStruct((n, pw), F32)
    in_specs = [row(dm), const((1, dm)), pl.BlockSpec((dm, in_cols), lambda i: (0, 0), pipeline_mode=pl.Buffered(1))]
    args = [x, g, w_bf]
    scratch = []
    tiles_per_seq = 1
    if mode == "prompt":
        assert seq % tm == 0
        tiles_per_seq = seq // tm
        kv_ext = jax.ShapeDtypeStruct((n // seq, N_META + seq, dm), F32)
        hbm = pl.BlockSpec(memory_space=pl.ANY)
        in_specs += [const((N_META, dm)), const((N_META, dm))]
        args += list(meta_kv)
        out_shape = [kv_ext, kv_ext, u_shape, b_full, b_full, b_heads, b_heads, b_heads]
        out_specs = [hbm, hbm, row(pw), row(dm), row(dm), heads, heads, heads]
        scratch = [pltpu.VMEM((2, tm, dm), F32), pltpu.VMEM((2, tm, dm), F32),
                   pltpu.SemaphoreType.DMA((2,)), pltpu.SemaphoreType.DMA((1,))]
    elif mode == "sample":
        out_shape = [f_full, f_full, u_shape, b_full, b_full, f_full]
        out_specs = [row(dm), row(dm), row(pw), row(dm), row(dm), row(dm)]
    else:
        out_shape = [f_full, f_full, u_shape, b_heads, b_heads]
        out_specs = [row(dm), row(dm), row(pw), heads, heads]
    return pl.pallas_call(
        functools.partial(_inproj_kernel, dm=dm, pw=pw, n_heads=n_heads, mode=mode, tm=tm, tiles_per_seq=tiles_per_seq),
        grid=(n // tm,),
        in_specs=in_specs,
        out_specs=out_specs,
        out_shape=out_shape,
        scratch_shapes=scratch,
        compiler_params=pltpu.CompilerParams(dimension_semantics=("arbitrary",), vmem_limit_bytes=VMEM_LIMIT),
        name="inproj_" + mode,
    )(*args)


def _pattn_kernel(q_ref, k_ref, v_ref, km_ref, vm_ref, lq1_ref, lk1_ref, lq2_ref, lk2_ref, sg_ref, o_ref,
                  q2_s, mx_s, l_s, acc_s, *, lam_init, nq):
    lam = _lam(lq1_ref, lk1_ref, lq2_ref, lk2_ref, lam_init)
    for i in range(nq):
        _pattn_block(i, q_ref, k_ref, v_ref, km_ref, vm_ref, sg_ref, o_ref, q2_s, mx_s, l_s, acc_s, lam, lam_init)


def _pattn_block(i, q_ref, k_ref, v_ref, km_ref, vm_ref, sg_ref, o_ref, q2_s, mx_s, l_s, acc_s, lam, lam_init):
    q = q_ref[i * TQ:(i + 1) * TQ, :]
    lane = lax.broadcasted_iota(jnp.int32, q.shape, 1)
    zero = jnp.zeros_like(q)
    q2_s[0:TQ, :] = jnp.where(lane < DK, q, zero)
    q2_s[TQ:2 * TQ, :] = jnp.where(lane >= DK, q, zero)

    def scores(j):
        return _nt_dot(q2_s[...], k_ref[j * TK:(j + 1) * TK, :])

    def diag_scores():
        r_in = lax.broadcasted_iota(jnp.int32, (2 * TQ, TK), 0)
        c_in = lax.broadcasted_iota(jnp.int32, (2 * TQ, TK), 1)
        r_q = jnp.where(r_in >= TQ, r_in - TQ, r_in)
        return jnp.where(c_in <= r_q, scores(i), NEG_INF)

    def meta_scores():
        s = _nt_dot(q2_s[...], km_ref[...])
        col = lax.broadcasted_iota(jnp.int32, s.shape, 1)
        return jnp.where(col < N_META, s, NEG_INF)

    mx_s[...] = meta_scores()

    for j in range(i):
        mx_s[...] = jnp.maximum(mx_s[...], _fold(jnp.maximum, _lane_tiles(scores(j))))
    mx = jnp.maximum(mx_s[...], _fold(jnp.maximum, _lane_tiles(diag_scores())))
    mx_s[...] = jnp.broadcast_to(jnp.max(mx, axis=-1, keepdims=True), mx.shape)

    def accumulate(s, vc, first=False):
        m_b = mx_s[...]
        p_tiles = [jnp.exp2(t - m_b) for t in _lane_tiles(s)]
        psum = _fold(jnp.add, p_tiles)
        pv = _dot(jnp.concatenate([t.astype(BF16) for t in p_tiles], axis=-1), vc)
        if first:
            l_s[...] = psum
            acc_s[...] = pv
        else:
            l_s[...] += psum
            acc_s[...] += pv

    accumulate(meta_scores(), vm_ref[...], first=True)

    for j in range(i):
        accumulate(scores(j), v_ref[j * TK:(j + 1) * TK, :])
    accumulate(diag_scores(), v_ref[i * TK:(i + 1) * TK, :])

    o = acc_s[...] / jnp.sum(l_s[...], axis=-1, keepdims=True)
    d = o[:TQ] - lam * o[TQ:]
    o_ref[i * TQ:(i + 1) * TQ, :] = _head_out(d, sg_ref[...], lam_init).astype(o_ref.dtype)


def _prompt_attention(q_hm, kb_hm, vb_hm, kbm_hm, vbm_hm, lam_vecs, sg, *, batch, seq, lam_init):
    n_heads, n, _ = q_hm.shape
    assert seq % TQ == 0 and TQ == TK
    nq = seq // TQ
    vec = lambda c: pl.BlockSpec((1, c), lambda b, h: (0, 0))
    kv = pl.BlockSpec((None, seq, HEAD), lambda b, h: (h, b, 0))
    meta = pl.BlockSpec((None, LANES, HEAD), lambda b, h: (h, 0, 0))
    return pl.pallas_call(
        functools.partial(_pattn_kernel, lam_init=lam_init, nq=nq),
        grid=(batch, n_heads),
        in_specs=[kv, kv, kv, meta, meta, vec(DK), vec(DK), vec(DK), vec(DK), vec(HEAD)],
        out_specs=kv,
        out_shape=jax.ShapeDtypeStruct((n_heads, n, HEAD), BF16),
        scratch_shapes=[pltpu.VMEM((2 * TQ, HEAD), BF16), pltpu.VMEM((2 * TQ, LANES), F32),
                        pltpu.VMEM((2 * TQ, LANES), F32), pltpu.VMEM((2 * TQ, HEAD), F32)],
        compiler_params=pltpu.CompilerParams(dimension_semantics=("arbitrary",) * 2, vmem_limit_bytes=VMEM_LIMIT),
        name="prompt_attn",
    )(q_hm, kb_hm, vb_hm, kbm_hm, vbm_hm, *lam_vecs, sg)


def _sattn_kernel(pt_ref, ck_hbm, cv_hbm, q_ref, kn_ref, vn_ref, lq1_ref, lk1_ref, lq2_ref, lk2_ref, sg_ref, o_ref,
                  kbuf, vbuf, sem, wq_s, bias_s, kb_s, vb_s, s_s, m_s, l_s, acc_s,
                  *, pps, page, n_heads, dec_seq, lam_init, n_b, n_j):
    b = pl.program_id(0)
    j = pl.program_id(1)
    n_steps = n_b * n_j
    step = b * n_j + j
    slot = step % PAGE_SLOTS
    rows = n_heads * 2 * dec_seq
    pr = page * n_heads
    ahead = PAGE_SLOTS - 1

    def page_copies(s_idx, sl):
        bb = lax.div(s_idx, jnp.int32(n_j))
        jj = lax.rem(s_idx, jnp.int32(n_j))
        copies = []
        for p in range(pps):
            pg = pt_ref[bb, jj * pps + p]
            copies.append(pltpu.make_async_copy(ck_hbm.at[pg], kbuf.at[sl, p], sem.at[sl]))
            copies.append(pltpu.make_async_copy(cv_hbm.at[pg], vbuf.at[sl, p], sem.at[sl]))
        return copies

    @pl.when(step == 0)
    def _():
        for d in range(ahead):
            for c in page_copies(step + d, d):
                c.start()

    @pl.when(step + ahead < n_steps)
    def _():
        for c in page_copies(step + ahead, (step + ahead) % PAGE_SLOTS):
            c.start()

    for c in page_copies(step, slot):
        c.wait()

    @pl.when(j == 0)
    def _():
        q = q_ref[0]
        lane = lax.broadcasted_iota(jnp.int32, (dec_seq, HEAD), 1)
        blocks = []
        for h in range(n_heads):
            qh = q[:, h * HEAD:(h + 1) * HEAD]
            blocks += [jnp.where(lane < DK, qh, 0.0), jnp.where(lane >= DK, qh, 0.0)]
        wq_s[...] = jnp.concatenate(blocks, axis=0).astype(BF16)
        r = lax.broadcasted_iota(jnp.int32, (rows, LANES), 0)
        c = lax.broadcasted_iota(jnp.int32, (rows, LANES), 1)
        bias_s[...] = jnp.where(c % n_heads == r // (2 * dec_seq), 0.0, NEG_INF)
        m_s[...] = jnp.full(m_s.shape, NEG_INF, F32)
        l_s[...] = jnp.zeros(l_s.shape, F32)
        acc_s[...] = jnp.zeros(acc_s.shape, F32)

    def update(n_cols, extra_mask=None):
        bias = bias_s[...]
        tiles = []
        for t in range(n_cols // LANES):
            s_t = s_s[:, t * LANES:(t + 1) * LANES] + bias
            tiles.append(s_t if extra_mask is None else jnp.where(extra_mask, s_t, NEG_INF))
        m_prev = m_s[...]
        m_new = jnp.maximum(m_prev, jnp.max(_fold(jnp.maximum, tiles), axis=-1, keepdims=True))
        m_b = jnp.broadcast_to(m_new, (rows, LANES))
        corr = jnp.exp2(m_prev - m_new)
        p_tiles = [jnp.exp2(t - m_b) for t in tiles]
        l_s[...] = l_s[...] * corr + jnp.sum(_fold(jnp.add, p_tiles), axis=-1, keepdims=True)
        pb = jnp.concatenate([t.astype(BF16) for t in p_tiles], axis=-1)
        acc_s[...] = acc_s[...] * corr + _dot(pb, vb_s[0:n_cols, :])
        m_s[...] = m_new

    for p in range(pps):
        kb_s[p * pr:(p + 1) * pr, :] = kbuf[slot, p].astype(BF16)
        vb_s[p * pr:(p + 1) * pr, :] = vbuf[slot, p].astype(BF16)
        s_s[:, p * pr:(p + 1) * pr] = _nt_dot(wq_s[...], kb_s[p * pr:(p + 1) * pr, :])
    update(pps * pr)

    @pl.when(j == n_j - 1)
    def _():
        n_new = dec_seq * n_heads
        pad = jnp.zeros((LANES - n_new, HEAD), F32)
        kb_s[0:LANES, :] = jnp.concatenate([kn_ref[0], pad], axis=0).astype(BF16)
        vb_s[0:LANES, :] = jnp.concatenate([vn_ref[0], pad], axis=0).astype(BF16)
        s_s[:, 0:LANES] = _nt_dot(wq_s[...], kb_s[0:LANES, :])
        r = lax.broadcasted_iota(jnp.int32, (rows, LANES), 0)
        c = lax.broadcasted_iota(jnp.int32, (rows, LANES), 1)
        update(LANES, extra_mask=(c < n_new) & (c // n_heads <= r % dec_seq))

        lam = _lam(lq1_ref, lk1_ref, lq2_ref, lk2_ref, lam_init)
        o = acc_s[...] / l_s[...]
        outs = []
        for h in range(n_heads):
            base = h * 2 * dec_seq
            d = o[base:base + dec_seq] - lam * o[base + dec_seq:base + 2 * dec_seq]
            outs.append(_head_out(d, sg_ref[...], lam_init))
        o_ref[0] = jnp.concatenate(outs, axis=-1)


def _sample_attention(page_table, cache_k, cache_v, q_s, k_s, v_s, lam_vecs, sg, *, lam_init):
    _, n_phys, page, n_heads, _ = cache_k.shape
    dm = n_heads * HEAD
    dec_batch, n_pages = page_table.shape
    dec_seq = q_s.shape[0] // dec_batch
    pps = PAGES_PER_STEP
    pr = page * n_heads
    assert n_pages % pps == 0 and 2 * dec_seq * n_heads == LANES and dec_seq % 8 == 0
    assert dec_seq * n_heads <= LANES and LANES % n_heads == 0
    assert dec_batch * (n_pages // pps) >= PAGE_SLOTS
    ck = cache_k.reshape(n_phys, pr, HEAD)
    cv = cache_v.reshape(n_phys, pr, HEAD)
    q3 = q_s.reshape(dec_batch, dec_seq, dm)
    k3 = k_s.reshape(dec_batch, dec_seq * n_heads, HEAD)
    v3 = v_s.reshape(dec_batch, dec_seq * n_heads, HEAD)

    tok = pl.BlockSpec((1, dec_seq, dm), lambda b, j, pt: (b, 0, 0))
    new = pl.BlockSpec((1, dec_seq * n_heads, HEAD), lambda b, j, pt: (b, 0, 0))
    vec = lambda c: pl.BlockSpec((1, c), lambda b, j, pt: (0, 0))
    hbm = pl.BlockSpec(memory_space=pl.ANY)
    rows = n_heads * 2 * dec_seq
    grid_spec = pltpu.PrefetchScalarGridSpec(
        num_scalar_prefetch=1,
        grid=(dec_batch, n_pages // pps),
        in_specs=[hbm, hbm, tok, new, new, vec(DK), vec(DK), vec(DK), vec(DK), vec(HEAD)],
        out_specs=tok,
        scratch_shapes=[pltpu.VMEM((PAGE_SLOTS, pps, pr, HEAD), F32), pltpu.VMEM((PAGE_SLOTS, pps, pr, HEAD), F32),
                        pltpu.SemaphoreType.DMA((PAGE_SLOTS,)),
                        pltpu.VMEM((rows, HEAD), BF16), pltpu.VMEM((rows, LANES), F32),
                        pltpu.VMEM((pps * pr, HEAD), BF16), pltpu.VMEM((pps * pr, HEAD), BF16),
                        pltpu.VMEM((rows, pps * pr), F32),
                        pltpu.VMEM((rows, 1), F32), pltpu.VMEM((rows, 1), F32), pltpu.VMEM((rows, HEAD), F32)],
    )
    out = pl.pallas_call(
        functools.partial(_sattn_kernel, pps=pps, page=page, n_heads=n_heads, dec_seq=dec_seq, lam_init=lam_init,
                          n_b=dec_batch, n_j=n_pages // pps),
        grid_spec=grid_spec,
        out_shape=jax.ShapeDtypeStruct((dec_batch, dec_seq, dm), F32),
        compiler_params=pltpu.CompilerParams(dimension_semantics=("arbitrary", "arbitrary"), vmem_limit_bytes=VMEM_LIMIT),
        name="sample_attn",
    )(page_table, ck, cv, q3, k3, v3, *lam_vecs, sg)
    return out.reshape(dec_batch * dec_seq, dm)


def _route(logits):
    lane = lax.broadcasted_iota(jnp.int32, logits.shape, 1)
    lane_f = lane.astype(F32)
    big = float(LANES)

    def first_max(vals):
        mx = jnp.max(vals, axis=-1, keepdims=True)
        idx = jnp.min(jnp.where(vals == mx, lane_f, big), axis=-1, keepdims=True)
        return mx, idx

    gl = jnp.where(lane < N_GROUPS, logits, NEG_INF)
    gmax, gidx = first_max(gl)
    g_val = 1.0 / jnp.sum(jnp.exp(gl - gmax), axis=-1, keepdims=True)
    lo = N_GROUPS + EXPERTS_PER_GROUP * gidx
    in_group = (lane_f >= lo) & (lane_f < lo + EXPERTS_PER_GROUP)
    el = jnp.where(in_group, logits, NEG_INF)
    v1, i1 = first_max(el)
    v2, i2 = first_max(jnp.where(lane_f == i1, NEG_INF, el))
    t = jnp.exp(v2 - v1)
    w1 = g_val / (1.0 + t)
    w2 = g_val * t / (1.0 + t)
    return i1 - N_GROUPS, i2 - N_GROUPS, w1, w2


def _merge_kernel(*refs, n_own, **static):
    x1_ref = refs[-2]
    i = pl.program_id(0)

    @pl.when(i < n_own)
    def _():
        _merge_tile(i, *refs, **static)

    @pl.when(i >= n_own)
    def _():
        x1_ref[...] = jnp.zeros(x1_ref.shape, F32)


def _merge_tile(i, *refs, mode, tm, dm, pw, n_heads, tiles_per_seq):
    if mode == "prompt":
        (x_ref, a_ref, u_ref, uprev_ref, umeta_ref, sga_ref, sgb_ref, wp_ref, ps_ref, wo_ref, gf_ref,
         wrh_ref, wrl_ref, br_ref, x1_ref, route_ref) = refs
    else:
        (x_ref, a_ref, uext_ref, sga_ref, sgb_ref, wp_ref, ps_ref, wo_ref, gf_ref,
         wrh_ref, wrl_ref, br_ref, _, x1_ref, route_ref) = refs
    cg = pw // len(POOL_WINDOWS)

    if mode == "prompt":
        a = jnp.concatenate([a_ref[h] for h in range(n_heads)], axis=-1).astype(F32)
        first = (i % tiles_per_seq) == 0
        halo = jnp.where(first, umeta_ref[...], uprev_ref[...])
        u = u_ref[...]
        run = jnp.concatenate([halo, u], axis=0)
        pooled = []
        width = 1
        for g, w in enumerate(POOL_WINDOWS):
            while width < w:
                run = run + pltpu.roll(run, shift=width, axis=0)
                width *= 2
            sl = slice(g * cg, (g + 1) * cg)
            pooled.append(run[HALO:, sl] / float(w) - u[:, sl])
    else:
        a = a_ref[...]
        seqs, ext_len, _ = uext_ref.shape
        n_new = ext_len - POOL_HIST
        pooled = []
        for g, w in enumerate(POOL_WINDOWS):
            sl = slice(g * cg, (g + 1) * cg)
            win = uext_ref[:, pl.ds(POOL_HIST, n_new), sl]
            for dlt in range(1, w):
                win = win + uext_ref[:, pl.ds(POOL_HIST - dlt, n_new), sl]
            tokv = uext_ref[:, pl.ds(POOL_HIST, n_new), sl]
            pooled.append((win / float(w) - tokv).reshape(seqs * n_new, cg))

    b = jnp.concatenate([_dot(pooled[g].astype(BF16), wp_ref[g]) for g in range(len(POOL_WINDOWS))], axis=-1)
    b = b * ps_ref[...]
    merged = sga_ref[...].astype(F32) * a + sgb_ref[...].astype(F32) * b
    x1 = x_ref[...] + _dot(merged.astype(BF16), wo_ref[...])
    x1_ref[...] = x1

    h = _rms(x1, gf_ref[...])
    h_hi = h.astype(BF16)
    h_lo = (h - h_hi.astype(F32)).astype(BF16)
    logits = (_dot(h_hi, wrh_ref[...]) + _dot(h_lo, wrh_ref[...]) + _dot(h_hi, wrl_ref[...])) + br_ref[...]
    e1, e2, w1, w2 = _route(logits)
    lane = lax.broadcasted_iota(jnp.int32, logits.shape, 1)
    route_ref[...] = jnp.where(lane == 0, e1, jnp.where(lane == 1, e2, jnp.where(lane == 2, w1, jnp.where(lane == 3, w2, 0.0))))


def _merge(x, a, u_parts, sga, sgb, wp_bf, ps, wo_bf, gf, wr_hi, wr_lo, br, *, mode, tm, seq, n_total, x1_buf=None):
    n, dm = x.shape
    pw = wp_bf.shape[0] * wp_bf.shape[1]
    n_heads = dm // HEAD
    assert n % tm == 0 and n_total % tm == 0
    n_own = n // tm
    own = lambda i: jnp.minimum(i, n_own - 1)
    row = lambda c: pl.BlockSpec((tm, c), lambda i: (own(i), 0))
    const = lambda shape: pl.BlockSpec(shape, lambda i: (0,) * len(shape))
    extra_specs, extra_args, aliases = [], [], {}
    n_steps = n_own
    if mode == "prompt":
        assert seq % tm == 0 and tm % HALO == 0
        u, u_meta = u_parts
        hb = tm // HALO
        a_spec = pl.BlockSpec((n_heads, tm, HEAD), lambda i: (0, own(i), 0))
        u_specs = [row(pw), pl.BlockSpec((HALO, pw), lambda i: (jnp.maximum(own(i) * hb - 1, 0), 0)), const((HALO, pw))]
        u_args = [u, u, u_meta]
        tiles_per_seq = seq // tm
        x1_spec = pl.BlockSpec((tm, dm), lambda i: (i, 0))
        n_steps = n_total // tm
    else:
        (u_ext,) = u_parts
        n_new = u_ext.shape[1] - POOL_HIST
        assert tm % n_new == 0
        a_spec = row(dm)
        u_specs = [pl.BlockSpec((tm // n_new, u_ext.shape[1], pw), lambda i: (i, 0, 0))]
        u_args = [u_ext]
        tiles_per_seq = 1
        first_tile = (n_total - n) // tm
        x1_spec = pl.BlockSpec((tm, dm), lambda i: (i + first_tile, 0))
        extra_specs, extra_args = [pl.BlockSpec(memory_space=pl.ANY)], [x1_buf]
        aliases = {12 + len(u_specs) - 1: 0}
    in_specs = [row(dm), a_spec, *u_specs, row(dm), row(dm),
                const(wp_bf.shape), const((1, dm)), const((dm, dm)), const((1, dm)),
                const((dm, LANES)), const((dm, LANES)), const((1, LANES)), *extra_specs]
    args = [x, a, *u_args, sga, sgb, wp_bf, ps, wo_bf, gf, wr_hi, wr_lo, br, *extra_args]
    if aliases:
        assert args[next(iter(aliases))] is x1_buf
    return pl.pallas_call(
        functools.partial(_merge_kernel, n_own=n_own, mode=mode, tm=tm, dm=dm, pw=pw, n_heads=n_heads,
                          tiles_per_seq=tiles_per_seq),
        grid=(n_steps,),
        in_specs=in_specs,
        out_specs=[x1_spec, row(LANES)],
        out_shape=[jax.ShapeDtypeStruct((n_total, dm), F32), jax.ShapeDtypeStruct((n, LANES), F32)],
        input_output_aliases=aliases,
        compiler_params=pltpu.CompilerParams(dimension_semantics=("arbitrary",), vmem_limit_bytes=VMEM_LIMIT),
        name="merge_" + mode,
    )(*args)


def _moe_kernel(ea_ref, eb_ref, nv_ref, pure_ref, tok_ref, tokn_ref, w_ref, x_hbm,
                wga_ref, wua_ref, wda_ref, wgb_ref, wub_ref, wdb_ref, gf_ref, gl_ref,
                op_hbm, os_hbm, xbuf, ybuf, gsem, ssem, *, tm, n_prompt):
    del ea_ref, eb_ref
    t = pl.program_id(0)
    n_t = pl.num_programs(0)
    slot = t % 2

    def gather(tok_smem, r, sl):
        return pltpu.make_async_copy(x_hbm.at[pl.ds(tok_smem[0, 0, r], 1)], xbuf.at[sl, pl.ds(r, 1)], gsem.at[sl])

    def scatter(out_hbm, r, sl, dst):
        return pltpu.make_async_copy(ybuf.at[sl, pl.ds(r, 1)], out_hbm.at[pl.ds(dst, 1)], ssem.at[sl])

    def counted(n, fn):
        def body(r, carry):
            fn(r)
            return carry
        lax.fori_loop(0, n, body, 0)

    def wait_scatters(sl, n):
        @pl.when(n == tm)
        def _():
            for r in range(tm):
                scatter(op_hbm, r, sl, 0).wait()

        @pl.when(n < tm)
        def _():
            counted(n, lambda r: scatter(op_hbm, r, sl, 0).wait())

    def start_scatters(sl, n, pure):
        fast = (n == tm) & (pure > 0)

        @pl.when(fast)
        def _():
            for r in range(tm):
                scatter(op_hbm, r, sl, tok_ref[0, 0, r]).start()

        @pl.when(jnp.logical_not(fast))
        def _():
            def one(r):
                tok = tok_ref[0, 0, r]

                @pl.when(tok < n_prompt)
                def _():
                    scatter(op_hbm, r, sl, tok).start()

                @pl.when(tok >= n_prompt)
                def _():
                    scatter(os_hbm, r, sl, tok - n_prompt).start()
            counted(n, one)

    nv = nv_ref[t]

    @pl.when((t == 0) & (nv > 0))
    def _():
        for r in range(tm):
            gather(tok_ref, r, 0).start()

    @pl.when(t >= 2)
    def _():
        wait_scatters(slot, nv_ref[jnp.maximum(t - 2, 0)])

    @pl.when((nv == 0) & (t >= 1) & (nv_ref[jnp.maximum(t - 1, 0)] > 0))
    def _():
        for r in range(tm):
            gather(tok_ref, r, slot).wait()

    @pl.when(nv > 0)
    def _():
        for r in range(tm):
            gather(tok_ref, r, slot).wait()
        for r in range(tm):
            gather(tokn_ref, r, 1 - slot).start()
        x = xbuf[slot]
        h = _rms(x, gf_ref[...]).astype(BF16)

        def expert(wg_ref, wu_ref, wd_ref):
            gate = _dot(h, wg_ref[...])
            up = _dot(h, wu_ref[...])
            act = gate * _sigmoid(gate) * up
            return _dot(act.astype(BF16), wd_ref[...])

        y = w_ref[:, 0:1] * expert(wga_ref, wua_ref, wda_ref)
        y = y + w_ref[:, 1:2] * expert(wgb_ref, wub_ref, wdb_ref)
        ybuf[slot] = _rms(x + y, gl_ref[...])
        start_scatters(slot, nv, pure_ref[t])

    @pl.when(t == n_t - 1)
    def _():
        wait_scatters(slot, nv)

        @pl.when(t >= 1)
        def _():
            wait_scatters(1 - slot, nv_ref[jnp.maximum(t - 1, 0)])


def _moe(x1, n_prompt, tile_ea, tile_eb, tile_nv, tile_pure, tok_tiles, w_tiles, wg_bf, wu_bf, wd_bf, gf, gl, *, tm):
    n_tok, dm = x1.shape
    n_steps = tile_ea.shape[0]
    de = wg_bf.shape[2]
    tok_spec = lambda off: pl.BlockSpec((1, 1, tm), lambda t, *_: (jnp.minimum(t + off, n_steps - 1), 0, 0),
                                        memory_space=pltpu.SMEM)
    wspec = lambda shape, which: pl.BlockSpec((None,) + shape, (lambda t, ea, eb, *_: (ea[t], 0, 0)) if which == 0
                                              else (lambda t, ea, eb, *_: (eb[t], 0, 0)))
    vec = pl.BlockSpec((1, dm), lambda t, *_: (0, 0))
    any_spec = pl.BlockSpec(memory_space=pl.ANY)
    grid_spec = pltpu.PrefetchScalarGridSpec(
        num_scalar_prefetch=4,
        grid=(n_steps,),
        in_specs=[tok_spec(0), tok_spec(1),
                  pl.BlockSpec((tm, 2), lambda t, *_: (t, 0)),
                  any_spec,
                  wspec((dm, de), 0), wspec((dm, de), 0), wspec((de, dm), 0),
                  wspec((dm, de), 1), wspec((dm, de), 1), wspec((de, dm), 1),
                  vec, vec],
        out_specs=[any_spec, any_spec],
        scratch_shapes=[pltpu.VMEM((2, tm, dm), F32), pltpu.VMEM((2, tm, dm), F32),
                        pltpu.SemaphoreType.DMA((2,)), pltpu.SemaphoreType.DMA((2,))],
    )
    return pl.pallas_call(
        functools.partial(_moe_kernel, tm=tm, n_prompt=n_prompt),
        grid_spec=grid_spec,
        out_shape=[jax.ShapeDtypeStruct((n_prompt, dm), F32), jax.ShapeDtypeStruct((n_tok - n_prompt, dm), F32)],
        compiler_params=pltpu.CompilerParams(dimension_semantics=("arbitrary",), vmem_limit_bytes=VMEM_LIMIT),
        name="moe",
    )(tile_ea, tile_eb, tile_nv, tile_pure, tok_tiles, tok_tiles, w_tiles, x1,
      wg_bf, wu_bf, wd_bf, wg_bf, wu_bf, wd_bf, gf, gl)


def _moe_schedule(route, n_prompt, *, tm):
    n = route.shape[0]
    e1 = route[:, 0].astype(jnp.int32)
    e2 = route[:, 1].astype(jnp.int32)
    swap = e2 < e1
    ea = jnp.where(swap, e2, e1)
    eb = jnp.where(swap, e1, e2)
    wa = jnp.where(swap, route[:, 3], route[:, 2])
    wb = jnp.where(swap, route[:, 2], route[:, 3])
    grp = ea // EXPERTS_PER_GROUP
    la = ea % EXPERTS_PER_GROUP
    lb = eb % EXPERTS_PER_GROUP
    cls = grp * N_PAIRS + la * (2 * EXPERTS_PER_GROUP - 1 - la) // 2 + (lb - la - 1)
    cls_sorted, order = lax.sort_key_val(cls, jnp.arange(n, dtype=jnp.int32))
    row_end = jnp.searchsorted(cls_sorted, jnp.arange(N_CLASSES, dtype=jnp.int32), side="right").astype(jnp.int32)
    row_start = jnp.concatenate([jnp.zeros((1,), jnp.int32), row_end[:-1]])
    counts = row_end - row_start
    tiles_per = (counts + tm - 1) // tm
    tile_end = jnp.cumsum(tiles_per)
    tile_start = tile_end - tiles_per
    n_tiles = n // tm + N_CLASSES + 1
    t = jnp.arange(n_tiles, dtype=jnp.int32)
    c_t = jnp.minimum(jnp.searchsorted(tile_end, t, side="right").astype(jnp.int32), N_CLASSES - 1)
    active = t < tile_end[-1]
    k_t = t - tile_start[c_t]
    left = jnp.where(active, counts[c_t] - k_t * tm, 0)
    nvalid = jnp.clip(left, 0, tm).astype(jnp.int32)
    last_c = c_t[jnp.maximum(tile_end[-1] - 1, 0)]
    c_eff = jnp.where(active, c_t, last_c)
    g_c = c_eff // N_PAIRS
    pair = c_eff % N_PAIRS
    la_tab, lb_tab = zip(*[(i, j) for i in range(EXPERTS_PER_GROUP) for j in range(i + 1, EXPERTS_PER_GROUP)])
    la_t = jnp.asarray(la_tab, jnp.int32)[pair]
    lb_t = jnp.asarray(lb_tab, jnp.int32)[pair]
    tile_ea = (g_c * EXPERTS_PER_GROUP + la_t).astype(jnp.int32)
    tile_eb = (g_c * EXPERTS_PER_GROUP + lb_t).astype(jnp.int32)
    r = jnp.arange(tm, dtype=jnp.int32)
    idx = row_start[c_t][:, None] + k_t[:, None] * tm + r[None, :]
    valid = r[None, :] < nvalid[:, None]
    tok = jnp.where(valid, order[jnp.clip(idx, 0, n - 1)], 0).astype(jnp.int32)
    w_tiles = jnp.stack([jnp.where(valid, wa[tok], 0.0), jnp.where(valid, wb[tok], 0.0)], axis=-1)
    pure = jnp.all(tok < n_prompt, axis=1).astype(jnp.int32)
    return tile_ea, tile_eb, nvalid, pure, tok.reshape(n_tiles, 1, tm), w_tiles.reshape(n_tiles * tm, 2)


def kernel(x_prompt, x_sample, cache_k, cache_v, state_pool, page_table, meta_tokens, norm_mix_g, w_in,
           lambda_q1, lambda_k1, lambda_q2, lambda_k2, subln_g, w_pool, pool_scale, w_out, norm_ffn_g,
           w_group, b_group, w_router, b_router, w_gate, w_up, w_down, norm_final_g):
    batch, seq, dm = x_prompt.shape
    dec_batch, dec_seq, _ = x_sample.shape
    depth = w_in.shape[0]
    assert depth == 1, "single-layer step"
    n_heads = dm // HEAD
    page = cache_k.shape[2]
    assert page_table.shape[1] * page >= POOL_HIST and N_META >= POOL_HIST + 1
    assert N_META <= HALO and w_group.shape[-1] == N_GROUPS and w_router.shape[-1] == EXPERTS_PER_GROUP
    lam_init = 0.8 - 0.6 * math.exp(-0.3 * 0)
    n_p, n_s = batch * seq, dec_batch * dec_seq
    n_tok = n_p + n_s
    pw = w_pool.shape[1] * w_pool.shape[2]

    w_in_bf = w_in[0].astype(BF16)
    g_mix = norm_mix_g[0].reshape(1, dm)
    lam_vecs = [v[0].reshape(1, DK) for v in (lambda_q1, lambda_k1, lambda_q2, lambda_k2)]
    sg = subln_g[0].reshape(1, HEAD)
    wp_bf = w_pool[0].astype(BF16)
    ps = pool_scale[0].reshape(1, dm)
    wo_bf = w_out[0].astype(BF16)
    g_ffn = norm_ffn_g[0].reshape(1, dm)
    g_fin = norm_final_g.reshape(1, dm)
    n_log = N_GROUPS + N_GROUPS * EXPERTS_PER_GROUP
    wr = jnp.concatenate([w_group[0], jnp.moveaxis(w_router[0], 0, 1).reshape(dm, -1)], axis=1)
    wr = jnp.pad(wr, ((0, 0), (0, LANES - n_log)))
    wr_hi = wr.astype(BF16)
    wr_lo = (wr - wr_hi.astype(F32)).astype(BF16)
    br = jnp.pad(jnp.concatenate([b_group[0], b_router[0].reshape(-1)]), (0, LANES - n_log)).reshape(1, LANES)
    wg_bf, wu_bf, wd_bf = w_gate[0].astype(BF16), w_up[0].astype(BF16), w_down[0].astype(BF16)

    xp = x_prompt.reshape(n_p, dm)
    xs = x_sample.reshape(n_s, dm)

    k_m, v_m, u_m, kb_m, vb_m = _inproj(meta_tokens, g_mix, w_in_bf, mode="meta", tm=N_META)
    k_ext, v_ext, u_p, sga_p, sgb_p, q_hm, kb_hm, vb_hm = _inproj(xp, g_mix, w_in_bf, mode="prompt", tm=TM_PROJ,
                                                                  seq=seq, meta_kv=(k_m, v_m))
    k_s, v_s, u_s, sga_s, sgb_s, q_s = _inproj(xs, g_mix, w_in_bf, mode="sample", tm=TM_PROJ)

    pad_meta = lambda t: jnp.pad(t, ((0, 0), (0, LANES - N_META), (0, 0)))
    a_hm = _prompt_attention(q_hm, kb_hm, vb_hm, pad_meta(kb_m), pad_meta(vb_m), lam_vecs, sg,
                             batch=batch, seq=seq, lam_init=lam_init)
    a_s = _sample_attention(page_table, cache_k, cache_v, q_s, k_s, v_s, lam_vecs, sg, lam_init=lam_init)

    u_halo = jnp.pad(u_m, ((HALO - N_META, 0), (0, 0)))
    x1, route_p = _merge(xp, a_hm, (u_p, u_halo), sga_p, sgb_p, wp_bf, ps, wo_bf, g_ffn, wr_hi, wr_lo, br,
                         mode="prompt", tm=TM_PROJ, seq=seq, n_total=n_tok)
    u_ext_s = jnp.concatenate([state_pool[0], u_s.reshape(dec_batch, dec_seq, pw)], axis=1)
    x1, route_s = _merge(xs, a_s, (u_ext_s,), sga_s, sgb_s, wp_bf, ps, wo_bf, g_ffn, wr_hi, wr_lo, br,
                         mode="sample", tm=TM_PROJ, seq=dec_seq, n_total=n_tok, x1_buf=x1)

    route = jnp.concatenate([route_p[:, :4], route_s[:, :4]], axis=0)
    tile_ea, tile_eb, tile_nv, tile_pure, tok_tiles, w_tiles = _moe_schedule(route, n_p, tm=TM_MOE)
    y_p, y_s = _moe(x1, n_p, tile_ea, tile_eb, tile_nv, tile_pure, tok_tiles, w_tiles, wg_bf, wu_bf, wd_bf,
                    g_ffn, g_fin, tm=TM_MOE)

    t_ext = N_META + seq
    new_k_p = k_ext.reshape(1, batch, t_ext, n_heads, HEAD)
    new_v_p = v_ext.reshape(1, batch, t_ext, n_heads, HEAD)
    new_pool_p = u_p.reshape(batch, seq, pw)[:, seq - POOL_HIST:][None]
    new_k_s = k_s.reshape(1, dec_batch, dec_seq, n_heads, HEAD)
    new_v_s = v_s.reshape(1, dec_batch, dec_seq, n_heads, HEAD)
    new_pool_s = u_ext_s[:, u_ext_s.shape[1] - POOL_HIST:][None]
    return (y_p.reshape(batch, seq, dm), y_s.reshape(dec_batch, dec_seq, dm),
            new_k_p, new_v_p, new_pool_p, new_k_s, new_v_s, new_pool_s)
```

```python
import functools
import math

import jax
import jax.numpy as jnp
from jax import lax
from jax.experimental import pallas as pl
from jax.experimental.pallas import tpu as pltpu

F32 = jnp.float32
BF16 = jnp.bfloat16

DK = 64
HEAD = 2 * DK
N_META = 16
POOL_WINDOWS = (2, 4, 8, 16)
POOL_HIST = max(POOL_WINDOWS) - 1
HALO = 16
N_GROUPS = 4
EXPERTS_PER_GROUP = 8
N_PAIRS = EXPERTS_PER_GROUP * (EXPERTS_PER_GROUP - 1) // 2
N_CLASSES = N_GROUPS * N_PAIRS
EPS = 1e-6
LOG2E = 1.4426950408889634
Q_SCALE = DK ** -0.5 * LOG2E
LANES = 128
NEG_INF = float("-inf")
VMEM_LIMIT = 56 * 1024 * 1024

TM_PROJ = 512
TQ = 512
TK = 512
PAGES_PER_STEP = 8
PAGE_SLOTS = 3
TM_MOE = 128


def _nt_dot(a, b):
    return lax.dot_general(a, b, (((1,), (1,)), ((), ())), preferred_element_type=F32)


def _dot(a, b):
    return jnp.dot(a, b, preferred_element_type=F32)


def _rms(x, g):
    r = lax.rsqrt(jnp.mean(x * x, axis=-1, keepdims=True) + EPS)
    return x * r * g


def _lam(lq1_ref, lk1_ref, lq2_ref, lk2_ref, lam_init):
    a = jnp.sum(lq1_ref[...] * lk1_ref[...], axis=-1, keepdims=True)
    b = jnp.sum(lq2_ref[...] * lk2_ref[...], axis=-1, keepdims=True)
    return jnp.exp(a) - jnp.exp(b) + lam_init


def _head_out(d, sg, lam_init):
    r = lax.rsqrt(jnp.mean(d * d, axis=-1, keepdims=True) + EPS)
    return d * r * sg * (1.0 - lam_init)


def _sigmoid(x):
    return 1.0 / (1.0 + jnp.exp(-x))


def _lane_tiles(s):
    return [s[:, t * LANES:(t + 1) * LANES] for t in range(s.shape[1] // LANES)]


def _fold(op, parts):
    parts = list(parts)
    while len(parts) > 1:
        parts = [op(parts[a], parts[a + 1]) if a + 1 < len(parts) else parts[a] for a in range(0, len(parts), 2)]
    return parts[0]


def _inproj_kernel(*refs, dm, pw, n_heads, mode, tm, tiles_per_seq):
    if mode == "prompt":
        (x_ref, g_ref, w_ref, kmeta_ref, vmeta_ref, kout_hbm, vout_hbm, u_ref, sga_ref, sgb_ref, q_ref, kb_ref, vb_ref,
         kst, vst, sem, msem) = refs
    elif mode == "sample":
        x_ref, g_ref, w_ref, k_ref, v_ref, u_ref, sga_ref, sgb_ref, q_ref = refs
    else:
        x_ref, g_ref, w_ref, k_ref, v_ref, u_ref, kb_ref, vb_ref = refs

    h = _rms(x_ref[...], g_ref[...]).astype(BF16)

    def proj(c0, n):
        return _dot(h, w_ref[:, c0:c0 + n])

    def store_heads(ref, val):
        for hd in range(n_heads):
            ref[hd] = val[:, hd * HEAD:(hd + 1) * HEAD]

    if mode == "prompt":
        i = pl.program_id(0)
        n_i = pl.num_programs(0)
        slot = i % 2

        def tile_copies(step, sl):
            bb = lax.div(step, jnp.int32(tiles_per_seq))
            rows = pl.ds(N_META + lax.rem(step, jnp.int32(tiles_per_seq)) * tm, tm)
            return [pltpu.make_async_copy(kst.at[sl], kout_hbm.at[bb, rows], sem.at[sl]),
                    pltpu.make_async_copy(vst.at[sl], vout_hbm.at[bb, rows], sem.at[sl])]

        def meta_copies(step):
            bb = lax.div(step, jnp.int32(tiles_per_seq))
            return [pltpu.make_async_copy(kmeta_ref, kout_hbm.at[bb, pl.ds(0, N_META)], msem.at[0]),
                    pltpu.make_async_copy(vmeta_ref, vout_hbm.at[bb, pl.ds(0, N_META)], msem.at[0])]

        @pl.when(lax.rem(i, jnp.int32(tiles_per_seq)) == 0)
        def _():
            for c in meta_copies(i):
                c.start()

        @pl.when(i >= 2)
        def _():
            for c in tile_copies(i - 2, slot):
                c.wait()

    if mode == "prompt":
        store_heads(q_ref, (proj(0, dm) * Q_SCALE).astype(BF16))
    elif mode == "sample":
        q_ref[...] = proj(0, dm) * Q_SCALE
    k = proj(dm, dm)
    v = proj(2 * dm, dm)
    if mode == "prompt":
        kst[slot] = k
        vst[slot] = v
        for c in tile_copies(i, slot):
            c.start()
    else:
        k_ref[...] = k
        v_ref[...] = v
    if mode != "sample":
        store_heads(kb_ref, k.astype(BF16))
        store_heads(vb_ref, v.astype(BF16))
    u_ref[...] = proj(3 * dm, pw)
    if mode != "meta":
        sga_ref[...] = _sigmoid(proj(3 * dm + pw, dm)).astype(BF16)
        sgb_ref[...] = _sigmoid(proj(4 * dm + pw, dm)).astype(BF16)

    if mode == "prompt":
        @pl.when(lax.rem(i, jnp.int32(tiles_per_seq)) == 0)
        def _():
            for c in meta_copies(i):
                c.wait()

        @pl.when(i == n_i - 1)
        def _():
            for c in tile_copies(i, slot):
                c.wait()

            @pl.when(i >= 1)
            def _():
                for c in tile_copies(i - 1, 1 - slot):
                    c.wait()


def _inproj(x, g, w_bf, *, mode, tm, seq=None, meta_kv=None):
    n, dm = x.shape
    in_cols = w_bf.shape[1]
    pw = in_cols - 5 * dm
    n_heads = dm // HEAD
    assert n % tm == 0
    row = lambda c: pl.BlockSpec((tm, c), lambda i: (i, 0))
    const = lambda shape: pl.BlockSpec(shape, lambda i: (0,) * len(shape))
    heads = pl.BlockSpec((n_heads, tm, HEAD), lambda i: (0, i, 0))
    f_full = jax.ShapeDtypeStruct((n, dm), F32)
    b_full = jax.ShapeDtypeStruct((n, dm), BF16)
    b_heads = jax.ShapeDtypeStruct((n_heads, n, HEAD), BF16)
    u_shape = jax.ShapeDtypeStruct((n, pw), F32)
    in_specs = [row(dm), const((1, dm)), pl.BlockSpec((dm, in_cols), lambda i: (0, 0), pipeline_mode=pl.Buffered(1))]
    args = [x, g, w_bf]
    scratch = []
    tiles_per_seq = 1
    if mode == "prompt":
        assert seq % tm == 0
        tiles_per_seq = seq // tm
        kv_ext = jax.ShapeDtypeStruct((n // seq, N_META + seq, dm), F32)
        hbm = pl.BlockSpec(memory_space=pl.ANY)
        in_specs += [const((N_META, dm)), const((N_META, dm))]
        args += list(meta_kv)
        out_shape = [kv_ext, kv_ext, u_shape, b_full, b_full, b_heads, b_heads, b_heads]
        out_specs = [hbm, hbm, row(pw), row(dm), row(dm), heads, heads, heads]
        scratch = [pltpu.VMEM((2, tm, dm), F32), pltpu.VMEM((2, tm, dm), F32),
                   pltpu.SemaphoreType.DMA((2,)), pltpu.SemaphoreType.DMA((1,))]
    elif mode == "sample":
        out_shape = [f_full, f_full, u_shape, b_full, b_full, f_full]
        out_specs = [row(dm), row(dm), row(pw), row(dm), row(dm), row(dm)]
    else:
        out_shape = [f_full, f_full, u_shape, b_heads, b_heads]
        out_specs = [row(dm), row(dm), row(pw), heads, heads]
    return pl.pallas_call(
        functools.partial(_inproj_kernel, dm=dm, pw=pw, n_heads=n_heads, mode=mode, tm=tm, tiles_per_seq=tiles_per_seq),
        grid=(n // tm,),
        in_specs=in_specs,
        out_specs=out_specs,
        out_shape=out_shape,
        scratch_shapes=scratch,
        compiler_params=pltpu.CompilerParams(dimension_semantics=("arbitrary",), vmem_limit_bytes=VMEM_LIMIT),
        name="inproj_" + mode,
    )(*args)


def _pattn_kernel(q_ref, k_ref, v_ref, km_ref, vm_ref, lq1_ref, lk1_ref, lq2_ref, lk2_ref, sg_ref, o_ref,
                  q2_s, mx_s, l_s, acc_s, *, lam_init, nq):
    lam = _lam(lq1_ref, lk1_ref, lq2_ref, lk2_ref, lam_init)
    for i in range(nq):
        _pattn_block(i, q_ref, k_ref, v_ref, km_ref, vm_ref, sg_ref, o_ref, q2_s, mx_s, l_s, acc_s, lam, lam_init)


def _pattn_block(i, q_ref, k_ref, v_ref, km_ref, vm_ref, sg_ref, o_ref, q2_s, mx_s, l_s, acc_s, lam, lam_init):
    q = q_ref[i * TQ:(i + 1) * TQ, :]
    lane = lax.broadcasted_iota(jnp.int32, q.shape, 1)
    zero = jnp.zeros_like(q)
    q2_s[0:TQ, :] = jnp.where(lane < DK, q, zero)
    q2_s[TQ:2 * TQ, :] = jnp.where(lane >= DK, q, zero)

    def scores(j):
        return _nt_dot(q2_s[...], k_ref[j * TK:(j + 1) * TK, :])

    def diag_scores():
        r_in = lax.broadcasted_iota(jnp.int32, (2 * TQ, TK), 0)
        c_in = lax.broadcasted_iota(jnp.int32, (2 * TQ, TK), 1)
        r_q = jnp.where(r_in >= TQ, r_in - TQ, r_in)
        return jnp.where(c_in <= r_q, scores(i), NEG_INF)

    def meta_scores():
        s = _nt_dot(q2_s[...], km_ref[...])
        col = lax.broadcasted_iota(jnp.int32, s.shape, 1)
        return jnp.where(col < N_META, s, NEG_INF)

    def accumulate(s, vc, first=False):
        tiles = _lane_tiles(s)
        cm = jnp.max(_fold(jnp.maximum, tiles), axis=-1, keepdims=True)
        m_b = jnp.broadcast_to(cm, (2 * TQ, LANES))
        if not first:
            m_old = mx_s[...]
            m_b = jnp.maximum(m_old, m_b)
            corr = jnp.exp2(m_old - m_b)
        p_tiles = [jnp.exp2(t - m_b) for t in tiles]
        psum = _fold(jnp.add, p_tiles)
        pv = _dot(jnp.concatenate([t.astype(BF16) for t in p_tiles], axis=-1), vc)
        if first:
            l_s[...] = psum
            acc_s[...] = pv
        else:
            l_s[...] = l_s[...] * corr + psum
            acc_s[...] = acc_s[...] * corr + pv
        mx_s[...] = m_b

    accumulate(meta_scores(), vm_ref[...], first=True)

    for j in range(i):
        accumulate(scores(j), v_ref[j * TK:(j + 1) * TK, :])
    accumulate(diag_scores(), v_ref[i * TK:(i + 1) * TK, :])

    o = acc_s[...] / jnp.sum(l_s[...], axis=-1, keepdims=True)
    d = o[:TQ] - lam * o[TQ:]
    o_ref[i * TQ:(i + 1) * TQ, :] = _head_out(d, sg_ref[...], lam_init).astype(o_ref.dtype)


def _prompt_attention(q_hm, kb_hm, vb_hm, kbm_hm, vbm_hm, lam_vecs, sg, *, batch, seq, lam_init):
    n_heads, n, _ = q_hm.shape
    assert seq % TQ == 0 and TQ == TK
    nq = seq // TQ
    vec = lambda c: pl.BlockSpec((1, c), lambda b, h: (0, 0))
    kv = pl.BlockSpec((None, seq, HEAD), lambda b, h: (h, b, 0))
    meta = pl.BlockSpec((None, LANES, HEAD), lambda b, h: (h, 0, 0))
    return pl.pallas_call(
        functools.partial(_pattn_kernel, lam_init=lam_init, nq=nq),
        grid=(batch, n_heads),
        in_specs=[kv, kv, kv, meta, meta, vec(DK), vec(DK), vec(DK), vec(DK), vec(HEAD)],
        out_specs=kv,
        out_shape=jax.ShapeDtypeStruct((n_heads, n, HEAD), BF16),
        scratch_shapes=[pltpu.VMEM((2 * TQ, HEAD), BF16), pltpu.VMEM((2 * TQ, LANES), F32),
                        pltpu.VMEM((2 * TQ, LANES), F32), pltpu.VMEM((2 * TQ, HEAD), F32)],
        compiler_params=pltpu.CompilerParams(dimension_semantics=("arbitrary",) * 2, vmem_limit_bytes=VMEM_LIMIT),
        name="prompt_attn",
    )(q_hm, kb_hm, vb_hm, kbm_hm, vbm_hm, *lam_vecs, sg)


def _sattn_kernel(pt_ref, ck_hbm, cv_hbm, q_ref, kn_ref, vn_ref, lq1_ref, lk1_ref, lq2_ref, lk2_ref, sg_ref, o_ref,
                  kbuf, vbuf, sem, wq_s, bias_s, kb_s, vb_s, s_s, pb_s, m_s, l_s, acc_s,
                  *, pps, page, n_heads, dec_seq, lam_init, n_b, n_j):
    b = pl.program_id(0)
    j = pl.program_id(1)
    n_steps = n_b * n_j
    step = b * n_j + j
    slot = step % PAGE_SLOTS
    rows = n_heads * 2 * dec_seq
    pr = page * n_heads
    ahead = PAGE_SLOTS - 1

    def page_copies(s_idx, sl):
        bb = lax.div(s_idx, jnp.int32(n_j))
        jj = lax.rem(s_idx, jnp.int32(n_j))
        copies = []
        for p in range(pps):
            pg = pt_ref[bb, jj * pps + p]
            copies.append(pltpu.make_async_copy(ck_hbm.at[pg], kbuf.at[sl, p], sem.at[sl]))
            copies.append(pltpu.make_async_copy(cv_hbm.at[pg], vbuf.at[sl, p], sem.at[sl]))
        return copies

    @pl.when(step == 0)
    def _():
        for d in range(ahead):
            for c in page_copies(step + d, d):
                c.start()

    @pl.when(step + ahead < n_steps)
    def _():
        for c in page_copies(step + ahead, (step + ahead) % PAGE_SLOTS):
            c.start()

    for c in page_copies(step, slot):
        c.wait()

    @pl.when(j == 0)
    def _():
        q = q_ref[0]
        lane = lax.broadcasted_iota(jnp.int32, (dec_seq, HEAD), 1)
        blocks = []
        for h in range(n_heads):
            qh = q[:, h * HEAD:(h + 1) * HEAD]
            blocks += [jnp.where(lane < DK, qh, 0.0), jnp.where(lane >= DK, qh, 0.0)]
        wq_s[...] = jnp.concatenate(blocks, axis=0).astype(BF16)
        r = lax.broadcasted_iota(jnp.int32, (rows, LANES), 0)
        c = lax.broadcasted_iota(jnp.int32, (rows, LANES), 1)
        bias_s[...] = jnp.where(c % n_heads == r // (2 * dec_seq), 0.0, NEG_INF)
        m_s[...] = jnp.full(m_s.shape, NEG_INF, F32)
        l_s[...] = jnp.zeros(l_s.shape, F32)
        acc_s[...] = jnp.zeros(acc_s.shape, F32)

    def update(n_cols, extra_mask=None):
        bias = bias_s[...]
        n_t = n_cols // LANES

        def masked(t):
            s_t = s_s[:, t * LANES:(t + 1) * LANES] + bias
            return s_t if extra_mask is None else jnp.where(extra_mask, s_t, NEG_INF)

        mx = masked(0)
        for t in range(1, n_t):
            mx = jnp.maximum(mx, masked(t))
        m_prev = m_s[...]
        m_new = jnp.maximum(m_prev, jnp.max(mx, axis=-1, keepdims=True))
        m_b = jnp.broadcast_to(m_new, (rows, LANES))
        corr = jnp.exp2(m_prev - m_new)
        psums = [None] * 4
        for t in range(n_t):
            s_t = (s_s[:, t * LANES:(t + 1) * LANES] - m_b) + bias
            p_t = jnp.exp2(s_t if extra_mask is None else jnp.where(extra_mask, s_t, NEG_INF))
            psums[t % 4] = p_t if psums[t % 4] is None else psums[t % 4] + p_t
            pb_s[:, t * LANES:(t + 1) * LANES] = p_t.astype(BF16)
        psum = _fold(jnp.add, [x for x in psums if x is not None])
        l_s[...] = l_s[...] * corr + jnp.sum(psum, axis=-1, keepdims=True)
        acc_s[...] = acc_s[...] * corr + _dot(pb_s[:, 0:n_cols], vb_s[0:n_cols, :])
        m_s[...] = m_new

    for p in range(pps):
        kb_s[p * pr:(p + 1) * pr, :] = kbuf[slot, p].astype(BF16)
        vb_s[p * pr:(p + 1) * pr, :] = vbuf[slot, p].astype(BF16)
        s_s[:, p * pr:(p + 1) * pr] = _nt_dot(wq_s[...], kb_s[p * pr:(p + 1) * pr, :])
    update(pps * pr)

    @pl.when(j == n_j - 1)
    def _():
        n_new = dec_seq * n_heads
        pad = jnp.zeros((LANES - n_new, HEAD), F32)
        kb_s[0:LANES, :] = jnp.concatenate([kn_ref[0], pad], axis=0).astype(BF16)
        vb_s[0:LANES, :] = jnp.concatenate([vn_ref[0], pad], axis=0).astype(BF16)
        s_s[:, 0:LANES] = _nt_dot(wq_s[...], kb_s[0:LANES, :])
        r = lax.broadcasted_iota(jnp.int32, (rows, LANES), 0)
        c = lax.broadcasted_iota(jnp.int32, (rows, LANES), 1)
        update(LANES, extra_mask=(c < n_new) & (c // n_heads <= r % dec_seq))

        lam = _lam(lq1_ref, lk1_ref, lq2_ref, lk2_ref, lam_init)
        o = acc_s[...] / l_s[...]
        outs = []
        for h in range(n_heads):
            base = h * 2 * dec_seq
            d = o[base:base + dec_seq] - lam * o[base + dec_seq:base + 2 * dec_seq]
            outs.append(_head_out(d, sg_ref[...], lam_init))
        o_ref[0] = jnp.concatenate(outs, axis=-1)


def _sample_attention(page_table, cache_k, cache_v, q_s, k_s, v_s, lam_vecs, sg, *, lam_init):
    _, n_phys, page, n_heads, _ = cache_k.shape
    dm = n_heads * HEAD
    dec_batch, n_pages = page_table.shape
    dec_seq = q_s.shape[0] // dec_batch
    pps = PAGES_PER_STEP
    pr = page * n_heads
    assert n_pages % pps == 0 and 2 * dec_seq * n_heads == LANES and dec_seq % 8 == 0
    assert dec_seq * n_heads <= LANES and LANES % n_heads == 0
    assert dec_batch * (n_pages // pps) >= PAGE_SLOTS
    ck = cache_k.reshape(n_phys, pr, HEAD)
    cv = cache_v.reshape(n_phys, pr, HEAD)
    q3 = q_s.reshape(dec_batch, dec_seq, dm)
    k3 = k_s.reshape(dec_batch, dec_seq * n_heads, HEAD)
    v3 = v_s.reshape(dec_batch, dec_seq * n_heads, HEAD)

    tok = pl.BlockSpec((1, dec_seq, dm), lambda b, j, pt: (b, 0, 0))
    new = pl.BlockSpec((1, dec_seq * n_heads, HEAD), lambda b, j, pt: (b, 0, 0))
    vec = lambda c: pl.BlockSpec((1, c), lambda b, j, pt: (0, 0))
    hbm = pl.BlockSpec(memory_space=pl.ANY)
    rows = n_heads * 2 * dec_seq
    grid_spec = pltpu.PrefetchScalarGridSpec(
        num_scalar_prefetch=1,
        grid=(dec_batch, n_pages // pps),
        in_specs=[hbm, hbm, tok, new, new, vec(DK), vec(DK), vec(DK), vec(DK), vec(HEAD)],
        out_specs=tok,
        scratch_shapes=[pltpu.VMEM((PAGE_SLOTS, pps, pr, HEAD), F32), pltpu.VMEM((PAGE_SLOTS, pps, pr, HEAD), F32),
                        pltpu.SemaphoreType.DMA((PAGE_SLOTS,)),
                        pltpu.VMEM((rows, HEAD), BF16), pltpu.VMEM((rows, LANES), F32),
                        pltpu.VMEM((pps * pr, HEAD), BF16), pltpu.VMEM((pps * pr, HEAD), BF16),
                        pltpu.VMEM((rows, pps * pr), F32), pltpu.VMEM((rows, pps * pr), BF16),
                        pltpu.VMEM((rows, 1), F32), pltpu.VMEM((rows, 1), F32), pltpu.VMEM((rows, HEAD), F32)],
    )
    out = pl.pallas_call(
        functools.partial(_sattn_kernel, pps=pps, page=page, n_heads=n_heads, dec_seq=dec_seq, lam_init=lam_init,
                          n_b=dec_batch, n_j=n_pages // pps),
        grid_spec=grid_spec,
        out_shape=jax.ShapeDtypeStruct((dec_batch, dec_seq, dm), F32),
        compiler_params=pltpu.CompilerParams(dimension_semantics=("arbitrary", "arbitrary"), vmem_limit_bytes=VMEM_LIMIT),
        name="sample_attn",
    )(page_table, ck, cv, q3, k3, v3, *lam_vecs, sg)
    return out.reshape(dec_batch * dec_seq, dm)


def _route(logits):
    lane = lax.broadcasted_iota(jnp.int32, logits.shape, 1)
    lane_f = lane.astype(F32)
    big = float(LANES)

    def first_max(vals):
        mx = jnp.max(vals, axis=-1, keepdims=True)
        idx = jnp.min(jnp.where(vals == mx, lane_f, big), axis=-1, keepdims=True)
        return mx, idx

    gl = jnp.where(lane < N_GROUPS, logits, NEG_INF)
    gmax, gidx = first_max(gl)
    g_val = 1.0 / jnp.sum(jnp.exp(gl - gmax), axis=-1, keepdims=True)
    lo = N_GROUPS + EXPERTS_PER_GROUP * gidx
    in_group = (lane_f >= lo) & (lane_f < lo + EXPERTS_PER_GROUP)
    el = jnp.where(in_group, logits, NEG_INF)
    v1, i1 = first_max(el)
    v2, i2 = first_max(jnp.where(lane_f == i1, NEG_INF, el))
    t = jnp.exp(v2 - v1)
    w1 = g_val / (1.0 + t)
    w2 = g_val * t / (1.0 + t)
    return i1 - N_GROUPS, i2 - N_GROUPS, w1, w2


def _merge_kernel(*refs, n_own, **static):
    x1_ref = refs[-2]
    i = pl.program_id(0)

    @pl.when(i < n_own)
    def _():
        _merge_tile(i, *refs, **static)

    @pl.when(i >= n_own)
    def _():
        x1_ref[...] = jnp.zeros(x1_ref.shape, F32)


def _merge_tile(i, *refs, mode, tm, dm, pw, n_heads, tiles_per_seq):
    if mode == "prompt":
        (x_ref, a_ref, u_ref, uprev_ref, umeta_ref, sga_ref, sgb_ref, wp_ref, ps_ref, wo_ref, gf_ref,
         wrh_ref, wrl_ref, br_ref, x1_ref, route_ref) = refs
    else:
        (x_ref, a_ref, uext_ref, sga_ref, sgb_ref, wp_ref, ps_ref, wo_ref, gf_ref,
         wrh_ref, wrl_ref, br_ref, _, x1_ref, route_ref) = refs
    cg = pw // len(POOL_WINDOWS)

    if mode == "prompt":
        a = jnp.concatenate([a_ref[h] for h in range(n_heads)], axis=-1).astype(F32)
        first = (i % tiles_per_seq) == 0
        halo = jnp.where(first, umeta_ref[...], uprev_ref[...])
        u = u_ref[...]
        run = jnp.concatenate([halo, u], axis=0)
        pooled = []
        width = 1
        for g, w in enumerate(POOL_WINDOWS):
            while width < w:
                run = run + pltpu.roll(run, shift=width, axis=0)
                width *= 2
            sl = slice(g * cg, (g + 1) * cg)
            pooled.append(run[HALO:, sl] / float(w) - u[:, sl])
    else:
        a = a_ref[...]
        seqs, ext_len, _ = uext_ref.shape
        n_new = ext_len - POOL_HIST
        pooled = []
        for g, w in enumerate(POOL_WINDOWS):
            sl = slice(g * cg, (g + 1) * cg)
            win = uext_ref[:, pl.ds(POOL_HIST, n_new), sl]
            for dlt in range(1, w):
                win = win + uext_ref[:, pl.ds(POOL_HIST - dlt, n_new), sl]
            tokv = uext_ref[:, pl.ds(POOL_HIST, n_new), sl]
            pooled.append((win / float(w) - tokv).reshape(seqs * n_new, cg))

    b = jnp.concatenate([_dot(pooled[g].astype(BF16), wp_ref[g]) for g in range(len(POOL_WINDOWS))], axis=-1)
    b = b * ps_ref[...]
    merged = sga_ref[...].astype(F32) * a + sgb_ref[...].astype(F32) * b
    x1 = x_ref[...] + _dot(merged.astype(BF16), wo_ref[...])
    x1_ref[...] = x1

    h = _rms(x1, gf_ref[...])
    h_hi = h.astype(BF16)
    h_lo = (h - h_hi.astype(F32)).astype(BF16)
    logits = (_dot(h_hi, wrh_ref[...]) + _dot(h_lo, wrh_ref[...]) + _dot(h_hi, wrl_ref[...])) + br_ref[...]
    e1, e2, w1, w2 = _route(logits)
    lane = lax.broadcasted_iota(jnp.int32, logits.shape, 1)
    route_ref[...] = jnp.where(lane == 0, e1, jnp.where(lane == 1, e2, jnp.where(lane == 2, w1, jnp.where(lane == 3, w2, 0.0))))


def _merge(x, a, u_parts, sga, sgb, wp_bf, ps, wo_bf, gf, wr_hi, wr_lo, br, *, mode, tm, seq, n_total, x1_buf=None):
    n, dm = x.shape
    pw = wp_bf.shape[0] * wp_bf.shape[1]
    n_heads = dm // HEAD
    assert n % tm == 0 and n_total % tm == 0
    n_own = n // tm
    own = lambda i: jnp.minimum(i, n_own - 1)
    row = lambda c: pl.BlockSpec((tm, c), lambda i: (own(i), 0))
    const = lambda shape: pl.BlockSpec(shape, lambda i: (0,) * len(shape))
    extra_specs, extra_args, aliases = [], [], {}
    n_steps = n_own
    if mode == "prompt":
        assert seq % tm == 0 and tm % HALO == 0
        u, u_meta = u_parts
        hb = tm // HALO
        a_spec = pl.BlockSpec((n_heads, tm, HEAD), lambda i: (0, own(i), 0))
        u_specs = [row(pw), pl.BlockSpec((HALO, pw), lambda i: (jnp.maximum(own(i) * hb - 1, 0), 0)), const((HALO, pw))]
        u_args = [u, u, u_meta]
        tiles_per_seq = seq // tm
        x1_spec = pl.BlockSpec((tm, dm), lambda i: (i, 0))
        n_steps = n_total // tm
    else:
        (u_ext,) = u_parts
        n_new = u_ext.shape[1] - POOL_HIST
        assert tm % n_new == 0
        a_spec = row(dm)
        u_specs = [pl.BlockSpec((tm // n_new, u_ext.shape[1], pw), lambda i: (i, 0, 0))]
        u_args = [u_ext]
        tiles_per_seq = 1
        first_tile = (n_total - n) // tm
        x1_spec = pl.BlockSpec((tm, dm), lambda i: (i + first_tile, 0))
        extra_specs, extra_args = [pl.BlockSpec(memory_space=pl.ANY)], [x1_buf]
        aliases = {12 + len(u_specs) - 1: 0}
    in_specs = [row(dm), a_spec, *u_specs, row(dm), row(dm),
                const(wp_bf.shape), const((1, dm)), const((dm, dm)), const((1, dm)),
                const((dm, LANES)), const((dm, LANES)), const((1, LANES)), *extra_specs]
    args = [x, a, *u_args, sga, sgb, wp_bf, ps, wo_bf, gf, wr_hi, wr_lo, br, *extra_args]
    if aliases:
        assert args[next(iter(aliases))] is x1_buf
    return pl.pallas_call(
        functools.partial(_merge_kernel, n_own=n_own, mode=mode, tm=tm, dm=dm, pw=pw, n_heads=n_heads,
                          tiles_per_seq=tiles_per_seq),
        grid=(n_steps,),
        in_specs=in_specs,
        out_specs=[x1_spec, row(LANES)],
        out_shape=[jax.ShapeDtypeStruct((n_total, dm), F32), jax.ShapeDtypeStruct((n, LANES), F32)],
        input_output_aliases=aliases,
        compiler_params=pltpu.CompilerParams(dimension_semantics=("arbitrary",), vmem_limit_bytes=VMEM_LIMIT),
        name="merge_" + mode,
    )(*args)


def _moe_kernel(ea_ref, eb_ref, nv_ref, pure_ref, tok_ref, tokn_ref, w_ref, x_hbm,
                wga_ref, wua_ref, wda_ref, wgb_ref, wub_ref, wdb_ref, gf_ref, gl_ref,
                op_hbm, os_hbm, xbuf, ybuf, gsem, ssem, *, tm, n_prompt):
    del ea_ref, eb_ref
    t = pl.program_id(0)
    n_t = pl.num_programs(0)
    slot = t % 2

    def gather(tok_smem, r, sl):
        return pltpu.make_async_copy(x_hbm.at[pl.ds(tok_smem[0, 0, r], 1)], xbuf.at[sl, pl.ds(r, 1)], gsem.at[sl])

    def scatter(out_hbm, r, sl, dst):
        return pltpu.make_async_copy(ybuf.at[sl, pl.ds(r, 1)], out_hbm.at[pl.ds(dst, 1)], ssem.at[sl])

    def counted(n, fn):
        def body(r, carry):
            fn(r)
            return carry
        lax.fori_loop(0, n, body, 0)

    def wait_scatters(sl, n):
        @pl.when(n == tm)
        def _():
            for r in range(tm):
                scatter(op_hbm, r, sl, 0).wait()

        @pl.when(n < tm)
        def _():
            counted(n, lambda r: scatter(op_hbm, r, sl, 0).wait())

    def start_scatters(sl, n, pure):
        fast = (n == tm) & (pure > 0)

        @pl.when(fast)
        def _():
            for r in range(tm):
                scatter(op_hbm, r, sl, tok_ref[0, 0, r]).start()

        @pl.when(jnp.logical_not(fast))
        def _():
            def one(r):
                tok = tok_ref[0, 0, r]

                @pl.when(tok < n_prompt)
                def _():
                    scatter(op_hbm, r, sl, tok).start()

                @pl.when(tok >= n_prompt)
                def _():
                    scatter(os_hbm, r, sl, tok - n_prompt).start()
            counted(n, one)

    nv = nv_ref[t]

    @pl.when((t == 0) & (nv > 0))
    def _():
        for r in range(tm):
            gather(tok_ref, r, 0).start()

    @pl.when(t >= 2)
    def _():
        wait_scatters(slot, nv_ref[jnp.maximum(t - 2, 0)])

    @pl.when((nv == 0) & (t >= 1) & (nv_ref[jnp.maximum(t - 1, 0)] > 0))
    def _():
        for r in range(tm):
            gather(tok_ref, r, slot).wait()

    @pl.when(nv > 0)
    def _():
        for r in range(tm):
            gather(tok_ref, r, slot).wait()
        for r in range(tm):
            gather(tokn_ref, r, 1 - slot).start()
        x = xbuf[slot]
        h = _rms(x, gf_ref[...]).astype(BF16)

        def expert(wg_ref, wu_ref, wd_ref):
            gate = _dot(h, wg_ref[...])
            up = _dot(h, wu_ref[...])
            act = gate * _sigmoid(gate) * up
            return _dot(act.astype(BF16), wd_ref[...])

        y = w_ref[:, 0:1] * expert(wga_ref, wua_ref, wda_ref)
        y = y + w_ref[:, 1:2] * expert(wgb_ref, wub_ref, wdb_ref)
        ybuf[slot] = _rms(x + y, gl_ref[...])
        start_scatters(slot, nv, pure_ref[t])

    @pl.when(t == n_t - 1)
    def _():
        wait_scatters(slot, nv)

        @pl.when(t >= 1)
        def _():
            wait_scatters(1 - slot, nv_ref[jnp.maximum(t - 1, 0)])


def _moe(x1, n_prompt, tile_ea, tile_eb, tile_nv, tile_pure, tok_tiles, w_tiles, wg_bf, wu_bf, wd_bf, gf, gl, *, tm):
    n_tok, dm = x1.shape
    n_steps = tile_ea.shape[0]
    de = wg_bf.shape[2]
    tok_spec = lambda off: pl.BlockSpec((1, 1, tm), lambda t, *_: (jnp.minimum(t + off, n_steps - 1), 0, 0),
                                        memory_space=pltpu.SMEM)
    wspec = lambda shape, which: pl.BlockSpec((None,) + shape, (lambda t, ea, eb, *_: (ea[t], 0, 0)) if which == 0
                                              else (lambda t, ea, eb, *_: (eb[t], 0, 0)))
    vec = pl.BlockSpec((1, dm), lambda t, *_: (0, 0))
    any_spec = pl.BlockSpec(memory_space=pl.ANY)
    grid_spec = pltpu.PrefetchScalarGridSpec(
        num_scalar_prefetch=4,
        grid=(n_steps,),
        in_specs=[tok_spec(0), tok_spec(1),
                  pl.BlockSpec((tm, 2), lambda t, *_: (t, 0)),
                  any_spec,
                  wspec((dm, de), 0), wspec((dm, de), 0), wspec((de, dm), 0),
                  wspec((dm, de), 1), wspec((dm, de), 1), wspec((de, dm), 1),
                  vec, vec],
        out_specs=[any_spec, any_spec],
        scratch_shapes=[pltpu.VMEM((2, tm, dm), F32), pltpu.VMEM((2, tm, dm), F32),
                        pltpu.SemaphoreType.DMA((2,)), pltpu.SemaphoreType.DMA((2,))],
    )
    return pl.pallas_call(
        functools.partial(_moe_kernel, tm=tm, n_prompt=n_prompt),
        grid_spec=grid_spec,
        out_shape=[jax.ShapeDtypeStruct((n_prompt, dm), F32), jax.ShapeDtypeStruct((n_tok - n_prompt, dm), F32)],
        compiler_params=pltpu.CompilerParams(dimension_semantics=("arbitrary",), vmem_limit_bytes=VMEM_LIMIT),
        name="moe",
    )(tile_ea, tile_eb, tile_nv, tile_pure, tok_tiles, tok_tiles, w_tiles, x1,
      wg_bf, wu_bf, wd_bf, wg_bf, wu_bf, wd_bf, gf, gl)


def _moe_schedule(route, n_prompt, *, tm):
    n = route.shape[0]
    e1 = route[:, 0].astype(jnp.int32)
    e2 = route[:, 1].astype(jnp.int32)
    swap = e2 < e1
    ea = jnp.where(swap, e2, e1)
    eb = jnp.where(swap, e1, e2)
    wa = jnp.where(swap, route[:, 3], route[:, 2])
    wb = jnp.where(swap, route[:, 2], route[:, 3])
    grp = ea // EXPERTS_PER_GROUP
    la = ea % EXPERTS_PER_GROUP
    lb = eb % EXPERTS_PER_GROUP
    cls = grp * N_PAIRS + la * (2 * EXPERTS_PER_GROUP - 1 - la) // 2 + (lb - la - 1)
    cls_sorted, order = lax.sort_key_val(cls, jnp.arange(n, dtype=jnp.int32))
    classes = jnp.arange(N_CLASSES, dtype=jnp.int32)
    row_end = jnp.sum((cls_sorted[None, :] <= classes[:, None]).astype(jnp.int32), axis=1)
    row_start = jnp.concatenate([jnp.zeros((1,), jnp.int32), row_end[:-1]])
    counts = row_end - row_start
    tiles_per = (counts + tm - 1) // tm
    tile_end = jnp.cumsum(tiles_per)
    tile_start = tile_end - tiles_per
    n_tiles = n // tm + N_CLASSES + 1
    t = jnp.arange(n_tiles, dtype=jnp.int32)
    c_t = jnp.minimum(jnp.sum((tile_end[None, :] <= t[:, None]).astype(jnp.int32), axis=1), N_CLASSES - 1)
    active = t < tile_end[-1]
    k_t = t - tile_start[c_t]
    left = jnp.where(active, counts[c_t] - k_t * tm, 0)
    nvalid = jnp.clip(left, 0, tm).astype(jnp.int32)
    last_c = c_t[jnp.maximum(tile_end[-1] - 1, 0)]
    c_eff = jnp.where(active, c_t, last_c)
    g_c = c_eff // N_PAIRS
    pair = c_eff % N_PAIRS
    la_tab, lb_tab = zip(*[(i, j) for i in range(EXPERTS_PER_GROUP) for j in range(i + 1, EXPERTS_PER_GROUP)])
    la_t = jnp.asarray(la_tab, jnp.int32)[pair]
    lb_t = jnp.asarray(lb_tab, jnp.int32)[pair]
    tile_ea = (g_c * EXPERTS_PER_GROUP + la_t).astype(jnp.int32)
    tile_eb = (g_c * EXPERTS_PER_GROUP + lb_t).astype(jnp.int32)
    r = jnp.arange(tm, dtype=jnp.int32)
    idx = row_start[c_t][:, None] + k_t[:, None] * tm + r[None, :]
    valid = r[None, :] < nvalid[:, None]
    tok = jnp.where(valid, order[jnp.clip(idx, 0, n - 1)], 0).astype(jnp.int32)
    w_tiles = jnp.stack([jnp.where(valid, wa[tok], 0.0), jnp.where(valid, wb[tok], 0.0)], axis=-1)
    pure = jnp.all(tok < n_prompt, axis=1).astype(jnp.int32)
    return tile_ea, tile_eb, nvalid, pure, tok.reshape(n_tiles, 1, tm), w_tiles.reshape(n_tiles * tm, 2)


def kernel(x_prompt, x_sample, cache_k, cache_v, state_pool, page_table, meta_tokens, norm_mix_g, w_in,
           lambda_q1, lambda_k1, lambda_q2, lambda_k2, subln_g, w_pool, pool_scale, w_out, norm_ffn_g,
           w_group, b_group, w_router, b_router, w_gate, w_up, w_down, norm_final_g):
    batch, seq, dm = x_prompt.shape
    dec_batch, dec_seq, _ = x_sample.shape
    depth = w_in.shape[0]
    assert depth == 1, "single-layer step"
    n_heads = dm // HEAD
    page = cache_k.shape[2]
    assert page_table.shape[1] * page >= POOL_HIST and N_META >= POOL_HIST + 1
    assert N_META <= HALO and w_group.shape[-1] == N_GROUPS and w_router.shape[-1] == EXPERTS_PER_GROUP
    lam_init = 0.8 - 0.6 * math.exp(-0.3 * 0)
    n_p, n_s = batch * seq, dec_batch * dec_seq
    n_tok = n_p + n_s
    pw = w_pool.shape[1] * w_pool.shape[2]

    w_in_bf = w_in[0].astype(BF16)
    g_mix = norm_mix_g[0].reshape(1, dm)
    lam_vecs = [v[0].reshape(1, DK) for v in (lambda_q1, lambda_k1, lambda_q2, lambda_k2)]
    sg = subln_g[0].reshape(1, HEAD)
    wp_bf = w_pool[0].astype(BF16)
    ps = pool_scale[0].reshape(1, dm)
    wo_bf = w_out[0].astype(BF16)
    g_ffn = norm_ffn_g[0].reshape(1, dm)
    g_fin = norm_final_g.reshape(1, dm)
    n_log = N_GROUPS + N_GROUPS * EXPERTS_PER_GROUP
    wr = jnp.concatenate([w_group[0], jnp.moveaxis(w_router[0], 0, 1).reshape(dm, -1)], axis=1)
    wr = jnp.pad(wr, ((0, 0), (0, LANES - n_log)))
    wr_hi = wr.astype(BF16)
    wr_lo = (wr - wr_hi.astype(F32)).astype(BF16)
    br = jnp.pad(jnp.concatenate([b_group[0], b_router[0].reshape(-1)]), (0, LANES - n_log)).reshape(1, LANES)
    wg_bf, wu_bf, wd_bf = w_gate[0].astype(BF16), w_up[0].astype(BF16), w_down[0].astype(BF16)

    xp = x_prompt.reshape(n_p, dm)
    xs = x_sample.reshape(n_s, dm)

    k_m, v_m, u_m, kb_m, vb_m = _inproj(meta_tokens, g_mix, w_in_bf, mode="meta", tm=N_META)
    k_ext, v_ext, u_p, sga_p, sgb_p, q_hm, kb_hm, vb_hm = _inproj(xp, g_mix, w_in_bf, mode="prompt", tm=TM_PROJ,
                                                                  seq=seq, meta_kv=(k_m, v_m))
    k_s, v_s, u_s, sga_s, sgb_s, q_s = _inproj(xs, g_mix, w_in_bf, mode="sample", tm=TM_PROJ)

    pad_meta = lambda t: jnp.pad(t, ((0, 0), (0, LANES - N_META), (0, 0)))
    a_hm = _prompt_attention(q_hm, kb_hm, vb_hm, pad_meta(kb_m), pad_meta(vb_m), lam_vecs, sg,
                             batch=batch, seq=seq, lam_init=lam_init)
    a_s = _sample_attention(page_table, cache_k, cache_v, q_s, k_s, v_s, lam_vecs, sg, lam_init=lam_init)

    u_halo = jnp.pad(u_m, ((HALO - N_META, 0), (0, 0)))
    x1, route_p = _merge(xp, a_hm, (u_p, u_halo), sga_p, sgb_p, wp_bf, ps, wo_bf, g_ffn, wr_hi, wr_lo, br,
                         mode="prompt", tm=TM_PROJ, seq=seq, n_total=n_tok)
    u_ext_s = jnp.concatenate([state_pool[0], u_s.reshape(dec_batch, dec_seq, pw)], axis=1)
    x1, route_s = _merge(xs, a_s, (u_ext_s,), sga_s, sgb_s, wp_bf, ps, wo_bf, g_ffn, wr_hi, wr_lo, br,
                         mode="sample", tm=TM_PROJ, seq=dec_seq, n_total=n_tok, x1_buf=x1)

    route = jnp.concatenate([route_p[:, :4], route_s[:, :4]], axis=0)
    tile_ea, tile_eb, tile_nv, tile_pure, tok_tiles, w_tiles = _moe_schedule(route, n_p, tm=TM_MOE)
    y_p, y_s = _moe(x1, n_p, tile_ea, tile_eb, tile_nv, tile_pure, tok_tiles, w_tiles, wg_bf, wu_bf, wd_bf,
                    g_ffn, g_fin, tm=TM_MOE)

    t_ext = N_META + seq
    new_k_p = k_ext.reshape(1, batch, t_ext, n_heads, HEAD)
    new_v_p = v_ext.reshape(1, batch, t_ext, n_heads, HEAD)
    new_pool_p = u_p.reshape(batch, seq, pw)[:, seq - POOL_HIST:][None]
    new_k_s = k_s.reshape(1, dec_batch, dec_seq, n_heads, HEAD)
    new_v_s = v_s.reshape(1, dec_batch, dec_seq, n_heads, HEAD)
    new_pool_s = u_ext_s[:, u_ext_s.shape[1] - POOL_HIST:][None]
    return (y_p.reshape(batch, seq, dm), y_s.reshape(dec_batch, dec_seq, dm),
            new_k_p, new_v_p, new_pool_p, new_k_s, new_v_s, new_pool_s)
```

```python
import functools
import math

import jax
import jax.numpy as jnp
from jax import lax
from jax.experimental import pallas as pl
from jax.experimental.pallas import tpu as pltpu

F32 = jnp.float32
BF16 = jnp.bfloat16

DK = 64
HEAD = 2 * DK
N_META = 16
POOL_WINDOWS = (2, 4, 8, 16)
POOL_HIST = max(POOL_WINDOWS) - 1
HALO = 16
N_GROUPS = 4
EXPERTS_PER_GROUP = 8
N_PAIRS = EXPERTS_PER_GROUP * (EXPERTS_PER_GROUP - 1) // 2
N_CLASSES = N_GROUPS * N_PAIRS
EPS = 1e-6
LOG2E = 1.4426950408889634
Q_SCALE = DK ** -0.5 * LOG2E
LANES = 128
NEG_INF = float("-inf")
VMEM_LIMIT = 56 * 1024 * 1024

TM_PROJ = 512
TQ = 512
TK = 512
PAGES_PER_STEP = 8
PAGE_SLOTS = 3
TM_MOE = 128


def _nt_dot(a, b):
    return lax.dot_general(a, b, (((1,), (1,)), ((), ())), preferred_element_type=F32)


def _dot(a, b):
    return jnp.dot(a, b, preferred_element_type=F32)


def _rms(x, g):
    r = lax.rsqrt(jnp.mean(x * x, axis=-1, keepdims=True) + EPS)
    return x * r * g


def _lam(lq1_ref, lk1_ref, lq2_ref, lk2_ref, lam_init):
    a = jnp.sum(lq1_ref[...] * lk1_ref[...], axis=-1, keepdims=True)
    b = jnp.sum(lq2_ref[...] * lk2_ref[...], axis=-1, keepdims=True)
    return jnp.exp(a) - jnp.exp(b) + lam_init


def _head_out(d, sg, lam_init):
    r = lax.rsqrt(jnp.mean(d * d, axis=-1, keepdims=True) + EPS)
    return d * r * sg * (1.0 - lam_init)


def _sigmoid(x):
    return 1.0 / (1.0 + jnp.exp(-x))


def _lane_tiles(s):
    return [s[:, t * LANES:(t + 1) * LANES] for t in range(s.shape[1] // LANES)]


def _fold(op, parts):
    parts = list(parts)
    while len(parts) > 1:
        parts = [op(parts[a], parts[a + 1]) if a + 1 < len(parts) else parts[a] for a in range(0, len(parts), 2)]
    return parts[0]


def _inproj_kernel(*refs, dm, pw, n_heads, mode, tm, tiles_per_seq):
    if mode == "prompt":
        (x_ref, g_ref, w_ref, kmeta_ref, vmeta_ref, kout_hbm, vout_hbm, u_ref, sga_ref, sgb_ref, q_ref, kb_ref, vb_ref,
         kst, vst, sem, msem) = refs
    elif mode == "sample":
        x_ref, g_ref, w_ref, k_ref, v_ref, u_ref, sga_ref, sgb_ref, q_ref = refs
    else:
        x_ref, g_ref, w_ref, k_ref, v_ref, u_ref, kb_ref, vb_ref = refs

    h = _rms(x_ref[...], g_ref[...]).astype(BF16)

    def proj(c0, n):
        return _dot(h, w_ref[:, c0:c0 + n])

    def store_heads(ref, val):
        for hd in range(n_heads):
            ref[hd] = val[:, hd * HEAD:(hd + 1) * HEAD]

    if mode == "prompt":
        i = pl.program_id(0)
        n_i = pl.num_programs(0)
        slot = i % 2

        def tile_copies(step, sl):
            bb = lax.div(step, jnp.int32(tiles_per_seq))
            rows = pl.ds(N_META + lax.rem(step, jnp.int32(tiles_per_seq)) * tm, tm)
            return [pltpu.make_async_copy(kst.at[sl], kout_hbm.at[bb, rows], sem.at[sl]),
                    pltpu.make_async_copy(vst.at[sl], vout_hbm.at[bb, rows], sem.at[sl])]

        def meta_copies(step):
            bb = lax.div(step, jnp.int32(tiles_per_seq))
            return [pltpu.make_async_copy(kmeta_ref, kout_hbm.at[bb, pl.ds(0, N_META)], msem.at[0]),
                    pltpu.make_async_copy(vmeta_ref, vout_hbm.at[bb, pl.ds(0, N_META)], msem.at[0])]

        @pl.when(lax.rem(i, jnp.int32(tiles_per_seq)) == 0)
        def _():
            for c in meta_copies(i):
                c.start()

        @pl.when(i >= 2)
        def _():
            for c in tile_copies(i - 2, slot):
                c.wait()

    if mode == "prompt":
        store_heads(q_ref, (proj(0, dm) * Q_SCALE).astype(BF16))
    elif mode == "sample":
        q_ref[...] = proj(0, dm) * Q_SCALE
    k = proj(dm, dm)
    v = proj(2 * dm, dm)
    if mode == "prompt":
        kst[slot] = k
        vst[slot] = v
        for c in tile_copies(i, slot):
            c.start()
    else:
        k_ref[...] = k
        v_ref[...] = v
    if mode != "sample":
        store_heads(kb_ref, k.astype(BF16))
        store_heads(vb_ref, v.astype(BF16))
    u_ref[...] = proj(3 * dm, pw)
    if mode != "meta":
        sga_ref[...] = _sigmoid(proj(3 * dm + pw, dm)).astype(BF16)
        sgb_ref[...] = _sigmoid(proj(4 * dm + pw, dm)).astype(BF16)

    if mode == "prompt":
        @pl.when(lax.rem(i, jnp.int32(tiles_per_seq)) == 0)
        def _():
            for c in meta_copies(i):
                c.wait()

        @pl.when(i == n_i - 1)
        def _():
            for c in tile_copies(i, slot):
                c.wait()

            @pl.when(i >= 1)
            def _():
                for c in tile_copies(i - 1, 1 - slot):
                    c.wait()


def _inproj(x, g, w_bf, *, mode, tm, seq=None, meta_kv=None):
    n, dm = x.shape
    in_cols = w_bf.shape[1]
    pw = in_cols - 5 * dm
    n_heads = dm // HEAD
    assert n % tm == 0
    row = lambda c: pl.BlockSpec((tm, c), lambda i: (i, 0))
    const = lambda shape: pl.BlockSpec(shape, lambda i: (0,) * len(shape))
    heads = pl.BlockSpec((n_heads, tm, HEAD), lambda i: (0, i, 0))
    f_full = jax.ShapeDtypeStruct((n, dm), F32)
    b_full = jax.ShapeDtypeStruct((n, dm), BF16)
    b_heads = jax.ShapeDtypeStruct((n_heads, n, HEAD), BF16)
    u_shape = jax.ShapeDtypeStruct((n, pw), F32)
    in_specs = [row(dm), const((1, dm)), pl.BlockSpec((dm, in_cols), lambda i: (0, 0), pipeline_mode=pl.Buffered(1))]
    args = [x, g, w_bf]
    scratch = []
    tiles_per_seq = 1
    if mode == "prompt":
        assert seq % tm == 0
        tiles_per_seq = seq // tm
        kv_ext = jax.ShapeDtypeStruct((n // seq, N_META + seq, dm), F32)
        hbm = pl.BlockSpec(memory_space=pl.ANY)
        in_specs += [const((N_META, dm)), const((N_META, dm))]
        args += list(meta_kv)
        out_shape = [kv_ext, kv_ext, u_shape, b_full, b_full, b_heads, b_heads, b_heads]
        out_specs = [hbm, hbm, row(pw), row(dm), row(dm), heads, heads, heads]
        scratch = [pltpu.VMEM((2, tm, dm), F32), pltpu.VMEM((2, tm, dm), F32),
                   pltpu.SemaphoreType.DMA((2,)), pltpu.SemaphoreType.DMA((1,))]
    elif mode == "sample":
        out_shape = [f_full, f_full, u_shape, b_full, b_full, f_full]
        out_specs = [row(dm), row(dm), row(pw), row(dm), row(dm), row(dm)]
    else:
        out_shape = [f_full, f_full, u_shape, b_heads, b_heads]
        out_specs = [row(dm), row(dm), row(pw), heads, heads]
    return pl.pallas_call(
        functools.partial(_inproj_kernel, dm=dm, pw=pw, n_heads=n_heads, mode=mode, tm=tm, tiles_per_seq=tiles_per_seq),
        grid=(n // tm,),
        in_specs=in_specs,
        out_specs=out_specs,
        out_shape=out_shape,
        scratch_shapes=scratch,
        compiler_params=pltpu.CompilerParams(dimension_semantics=("arbitrary",), vmem_limit_bytes=VMEM_LIMIT),
        name="inproj_" + mode,
    )(*args)


def _pattn_kernel(q_ref, k_ref, v_ref, km_ref, vm_ref, lq1_ref, lk1_ref, lq2_ref, lk2_ref, sg_ref, o_ref,
                  q2_s, mx_s, l_s, acc_s, *, lam_init, nq):
    lam = _lam(lq1_ref, lk1_ref, lq2_ref, lk2_ref, lam_init)
    for i in range(nq):
        _pattn_block(i, q_ref, k_ref, v_ref, km_ref, vm_ref, sg_ref, o_ref, q2_s, mx_s, l_s, acc_s, lam, lam_init)


def _pattn_block(i, q_ref, k_ref, v_ref, km_ref, vm_ref, sg_ref, o_ref, q2_s, mx_s, l_s, acc_s, lam, lam_init):
    q = q_ref[i * TQ:(i + 1) * TQ, :]
    lane = lax.broadcasted_iota(jnp.int32, q.shape, 1)
    zero = jnp.zeros_like(q)
    q2_s[0:TQ, :] = jnp.where(lane < DK, q, zero)
    q2_s[TQ:2 * TQ, :] = jnp.where(lane >= DK, q, zero)

    def scores(j):
        return _nt_dot(q2_s[...], k_ref[j * TK:(j + 1) * TK, :])

    def meta_scores():
        s = _nt_dot(q2_s[...], km_ref[...])
        col = lax.broadcasted_iota(jnp.int32, s.shape, 1)
        return jnp.where(col < N_META, s, NEG_INF)

    def accumulate(s, vc, rs=slice(0, 2 * TQ), first=False):
        tiles = _lane_tiles(s)
        cm = jnp.max(_fold(jnp.maximum, tiles), axis=-1, keepdims=True)
        m_b = jnp.broadcast_to(cm, (rs.stop - rs.start, LANES))
        if not first:
            m_old = mx_s[rs, :]
            m_b = jnp.maximum(m_old, m_b)
            corr = jnp.exp2(m_old - m_b)
        p_tiles = [jnp.exp2(t - m_b) for t in tiles]
        psum = _fold(jnp.add, p_tiles)
        pb = p_tiles[0].astype(BF16) if len(p_tiles) == 1 else jnp.concatenate([t.astype(BF16) for t in p_tiles], axis=-1)
        pv = _dot(pb, vc)
        if first:
            l_s[rs, :] = psum
            acc_s[rs, :] = pv
        else:
            l_s[rs, :] = l_s[rs, :] * corr + psum
            acc_s[rs, :] = acc_s[rs, :] * corr + pv
        mx_s[rs, :] = m_b

    accumulate(meta_scores(), vm_ref[...], first=True)

    for j in range(i):
        accumulate(scores(j), v_ref[j * TK:(j + 1) * TK, :])

    half = TK // 2
    k0 = i * TK
    r_in = lax.broadcasted_iota(jnp.int32, (2 * TQ, half), 0)
    c_in = lax.broadcasted_iota(jnp.int32, (2 * TQ, half), 1)
    r_q = jnp.where(r_in >= TQ, r_in - TQ, r_in)
    s_a = jnp.where(c_in <= r_q, _nt_dot(q2_s[...], k_ref[k0:k0 + half, :]), NEG_INF)
    accumulate(s_a, v_ref[k0:k0 + half, :])
    tri = lax.broadcasted_iota(jnp.int32, (half, half), 1) <= lax.broadcasted_iota(jnp.int32, (half, half), 0)
    for lo in (TQ - half, 2 * TQ - half):
        rs = slice(lo, lo + half)
        s_b = jnp.where(tri, _nt_dot(q2_s[rs, :], k_ref[k0 + half:k0 + TK, :]), NEG_INF)
        accumulate(s_b, v_ref[k0 + half:k0 + TK, :], rs=rs)

    o = acc_s[...] / jnp.sum(l_s[...], axis=-1, keepdims=True)
    d = o[:TQ] - lam * o[TQ:]
    o_ref[i * TQ:(i + 1) * TQ, :] = _head_out(d, sg_ref[...], lam_init).astype(o_ref.dtype)


def _prompt_attention(q_hm, kb_hm, vb_hm, kbm_hm, vbm_hm, lam_vecs, sg, *, batch, seq, lam_init):
    n_heads, n, _ = q_hm.shape
    assert seq % TQ == 0 and TQ == TK
    nq = seq // TQ
    vec = lambda c: pl.BlockSpec((1, c), lambda b, h: (0, 0))
    kv = pl.BlockSpec((None, seq, HEAD), lambda b, h: (h, b, 0))
    meta = pl.BlockSpec((None, LANES, HEAD), lambda b, h: (h, 0, 0))
    return pl.pallas_call(
        functools.partial(_pattn_kernel, lam_init=lam_init, nq=nq),
        grid=(batch, n_heads),
        in_specs=[kv, kv, kv, meta, meta, vec(DK), vec(DK), vec(DK), vec(DK), vec(HEAD)],
        out_specs=kv,
        out_shape=jax.ShapeDtypeStruct((n_heads, n, HEAD), BF16),
        scratch_shapes=[pltpu.VMEM((2 * TQ, HEAD), BF16), pltpu.VMEM((2 * TQ, LANES), F32),
                        pltpu.VMEM((2 * TQ, LANES), F32), pltpu.VMEM((2 * TQ, HEAD), F32)],
        compiler_params=pltpu.CompilerParams(dimension_semantics=("arbitrary",) * 2, vmem_limit_bytes=VMEM_LIMIT),
        name="prompt_attn",
    )(q_hm, kb_hm, vb_hm, kbm_hm, vbm_hm, *lam_vecs, sg)


def _sattn_kernel(pt_ref, ck_hbm, cv_hbm, q_ref, kn_ref, vn_ref, lq1_ref, lk1_ref, lq2_ref, lk2_ref, sg_ref, o_ref,
                  kbuf, vbuf, sem, wq_s, bias_s, kb_s, vb_s, s_s, pb_s, m_s, l_s, acc_s,
                  *, pps, page, n_heads, dec_seq, lam_init, n_b, n_j):
    b = pl.program_id(0)
    j = pl.program_id(1)
    n_steps = n_b * n_j
    step = b * n_j + j
    slot = step % PAGE_SLOTS
    rows = n_heads * 2 * dec_seq
    pr = page * n_heads
    ahead = PAGE_SLOTS - 1

    def page_copies(s_idx, sl):
        bb = lax.div(s_idx, jnp.int32(n_j))
        jj = lax.rem(s_idx, jnp.int32(n_j))
        copies = []
        for p in range(pps):
            pg = pt_ref[bb, jj * pps + p]
            copies.append(pltpu.make_async_copy(ck_hbm.at[pg], kbuf.at[sl, p], sem.at[sl]))
            copies.append(pltpu.make_async_copy(cv_hbm.at[pg], vbuf.at[sl, p], sem.at[sl]))
        return copies

    @pl.when(step == 0)
    def _():
        for d in range(ahead):
            for c in page_copies(step + d, d):
                c.start()

    @pl.when(step + ahead < n_steps)
    def _():
        for c in page_copies(step + ahead, (step + ahead) % PAGE_SLOTS):
            c.start()

    for c in page_copies(step, slot):
        c.wait()

    @pl.when(j == 0)
    def _():
        q = q_ref[0]
        lane = lax.broadcasted_iota(jnp.int32, (dec_seq, HEAD), 1)
        blocks = []
        for h in range(n_heads):
            qh = q[:, h * HEAD:(h + 1) * HEAD]
            blocks += [jnp.where(lane < DK, qh, 0.0), jnp.where(lane >= DK, qh, 0.0)]
        wq_s[...] = jnp.concatenate(blocks, axis=0).astype(BF16)
        r = lax.broadcasted_iota(jnp.int32, (rows, LANES), 0)
        c = lax.broadcasted_iota(jnp.int32, (rows, LANES), 1)
        bias_s[...] = jnp.where(c % n_heads == r // (2 * dec_seq), 0.0, NEG_INF)
        m_s[...] = jnp.full(m_s.shape, NEG_INF, F32)
        l_s[...] = jnp.zeros(l_s.shape, F32)
        acc_s[...] = jnp.zeros(acc_s.shape, F32)

    def update(n_cols, extra_mask=None):
        bias = bias_s[...]
        n_t = n_cols // LANES

        def masked(t):
            s_t = s_s[:, t * LANES:(t + 1) * LANES] + bias
            return s_t if extra_mask is None else jnp.where(extra_mask, s_t, NEG_INF)

        mx = masked(0)
        for t in range(1, n_t):
            mx = jnp.maximum(mx, masked(t))
        m_prev = m_s[...]
        m_new = jnp.maximum(m_prev, jnp.max(mx, axis=-1, keepdims=True))
        m_b = jnp.broadcast_to(m_new, (rows, LANES))
        corr = jnp.exp2(m_prev - m_new)
        psums = [None] * 4
        for t in range(n_t):
            s_t = (s_s[:, t * LANES:(t + 1) * LANES] - m_b) + bias
            p_t = jnp.exp2(s_t if extra_mask is None else jnp.where(extra_mask, s_t, NEG_INF))
            psums[t % 4] = p_t if psums[t % 4] is None else psums[t % 4] + p_t
            pb_s[:, t * LANES:(t + 1) * LANES] = p_t.astype(BF16)
        psum = _fold(jnp.add, [x for x in psums if x is not None])
        l_s[...] = l_s[...] * corr + jnp.sum(psum, axis=-1, keepdims=True)
        acc_s[...] = acc_s[...] * corr + _dot(pb_s[:, 0:n_cols], vb_s[0:n_cols, :])
        m_s[...] = m_new

    for p in range(pps):
        kb_s[p * pr:(p + 1) * pr, :] = kbuf[slot, p].astype(BF16)
        vb_s[p * pr:(p + 1) * pr, :] = vbuf[slot, p].astype(BF16)
        s_s[:, p * pr:(p + 1) * pr] = _nt_dot(wq_s[...], kb_s[p * pr:(p + 1) * pr, :])
    update(pps * pr)

    @pl.when(j == n_j - 1)
    def _():
        n_new = dec_seq * n_heads
        pad = jnp.zeros((LANES - n_new, HEAD), F32)
        kb_s[0:LANES, :] = jnp.concatenate([kn_ref[0], pad], axis=0).astype(BF16)
        vb_s[0:LANES, :] = jnp.concatenate([vn_ref[0], pad], axis=0).astype(BF16)
        s_s[:, 0:LANES] = _nt_dot(wq_s[...], kb_s[0:LANES, :])
        r = lax.broadcasted_iota(jnp.int32, (rows, LANES), 0)
        c = lax.broadcasted_iota(jnp.int32, (rows, LANES), 1)
        update(LANES, extra_mask=(c < n_new) & (c // n_heads <= r % dec_seq))

        lam = _lam(lq1_ref, lk1_ref, lq2_ref, lk2_ref, lam_init)
        o = acc_s[...] / l_s[...]
        outs = []
        for h in range(n_heads):
            base = h * 2 * dec_seq
            d = o[base:base + dec_seq] - lam * o[base + dec_seq:base + 2 * dec_seq]
            outs.append(_head_out(d, sg_ref[...], lam_init))
        o_ref[0] = jnp.concatenate(outs, axis=-1)


def _sample_attention(page_table, cache_k, cache_v, q_s, k_s, v_s, lam_vecs, sg, *, lam_init):
    _, n_phys, page, n_heads, _ = cache_k.shape
    dm = n_heads * HEAD
    dec_batch, n_pages = page_table.shape
    dec_seq = q_s.shape[0] // dec_batch
    pps = PAGES_PER_STEP
    pr = page * n_heads
    assert n_pages % pps == 0 and 2 * dec_seq * n_heads == LANES and dec_seq % 8 == 0
    assert dec_seq * n_heads <= LANES and LANES % n_heads == 0
    assert dec_batch * (n_pages // pps) >= PAGE_SLOTS
    ck = cache_k.reshape(n_phys, pr, HEAD)
    cv = cache_v.reshape(n_phys, pr, HEAD)
    q3 = q_s.reshape(dec_batch, dec_seq, dm)
    k3 = k_s.reshape(dec_batch, dec_seq * n_heads, HEAD)
    v3 = v_s.reshape(dec_batch, dec_seq * n_heads, HEAD)

    tok = pl.BlockSpec((1, dec_seq, dm), lambda b, j, pt: (b, 0, 0))
    new = pl.BlockSpec((1, dec_seq * n_heads, HEAD), lambda b, j, pt: (b, 0, 0))
    vec = lambda c: pl.BlockSpec((1, c), lambda b, j, pt: (0, 0))
    hbm = pl.BlockSpec(memory_space=pl.ANY)
    rows = n_heads * 2 * dec_seq
    grid_spec = pltpu.PrefetchScalarGridSpec(
        num_scalar_prefetch=1,
        grid=(dec_batch, n_pages // pps),
        in_specs=[hbm, hbm, tok, new, new, vec(DK), vec(DK), vec(DK), vec(DK), vec(HEAD)],
        out_specs=tok,
        scratch_shapes=[pltpu.VMEM((PAGE_SLOTS, pps, pr, HEAD), F32), pltpu.VMEM((PAGE_SLOTS, pps, pr, HEAD), F32),
                        pltpu.SemaphoreType.DMA((PAGE_SLOTS,)),
                        pltpu.VMEM((rows, HEAD), BF16), pltpu.VMEM((rows, LANES), F32),
                        pltpu.VMEM((pps * pr, HEAD), BF16), pltpu.VMEM((pps * pr, HEAD), BF16),
                        pltpu.VMEM((rows, pps * pr), F32), pltpu.VMEM((rows, pps * pr), BF16),
                        pltpu.VMEM((rows, 1), F32), pltpu.VMEM((rows, 1), F32), pltpu.VMEM((rows, HEAD), F32)],
    )
    out = pl.pallas_call(
        functools.partial(_sattn_kernel, pps=pps, page=page, n_heads=n_heads, dec_seq=dec_seq, lam_init=lam_init,
                          n_b=dec_batch, n_j=n_pages // pps),
        grid_spec=grid_spec,
        out_shape=jax.ShapeDtypeStruct((dec_batch, dec_seq, dm), F32),
        compiler_params=pltpu.CompilerParams(dimension_semantics=("arbitrary", "arbitrary"), vmem_limit_bytes=VMEM_LIMIT),
        name="sample_attn",
    )(page_table, ck, cv, q3, k3, v3, *lam_vecs, sg)
    return out.reshape(dec_batch * dec_seq, dm)


def _route(logits):
    lane = lax.broadcasted_iota(jnp.int32, logits.shape, 1)
    lane_f = lane.astype(F32)
    big = float(LANES)

    def first_max(vals):
        mx = jnp.max(vals, axis=-1, keepdims=True)
        idx = jnp.min(jnp.where(vals == mx, lane_f, big), axis=-1, keepdims=True)
        return mx, idx

    gl = jnp.where(lane < N_GROUPS, logits, NEG_INF)
    gmax, gidx = first_max(gl)
    g_val = 1.0 / jnp.sum(jnp.exp(gl - gmax), axis=-1, keepdims=True)
    lo = N_GROUPS + EXPERTS_PER_GROUP * gidx
    in_group = (lane_f >= lo) & (lane_f < lo + EXPERTS_PER_GROUP)
    el = jnp.where(in_group, logits, NEG_INF)
    v1, i1 = first_max(el)
    v2, i2 = first_max(jnp.where(lane_f == i1, NEG_INF, el))
    t = jnp.exp(v2 - v1)
    w1 = g_val / (1.0 + t)
    w2 = g_val * t / (1.0 + t)
    return i1 - N_GROUPS, i2 - N_GROUPS, w1, w2


def _merge_kernel(*refs, n_own, **static):
    x1_ref = refs[-2]
    i = pl.program_id(0)

    @pl.when(i < n_own)
    def _():
        _merge_tile(i, *refs, **static)

    @pl.when(i >= n_own)
    def _():
        x1_ref[...] = jnp.zeros(x1_ref.shape, F32)


def _merge_tile(i, *refs, mode, tm, dm, pw, n_heads, tiles_per_seq):
    if mode == "prompt":
        (x_ref, a_ref, u_ref, uprev_ref, umeta_ref, sga_ref, sgb_ref, wp_ref, ps_ref, wo_ref, gf_ref,
         wrh_ref, wrl_ref, br_ref, x1_ref, route_ref) = refs
    else:
        (x_ref, a_ref, uext_ref, sga_ref, sgb_ref, wp_ref, ps_ref, wo_ref, gf_ref,
         wrh_ref, wrl_ref, br_ref, _, x1_ref, route_ref) = refs
    cg = pw // len(POOL_WINDOWS)

    if mode == "prompt":
        a = jnp.concatenate([a_ref[h] for h in range(n_heads)], axis=-1).astype(F32)
        first = (i % tiles_per_seq) == 0
        halo = jnp.where(first, umeta_ref[...], uprev_ref[...])
        u = u_ref[...]
        run = jnp.concatenate([halo, u], axis=0)
        pooled = []
        width = 1
        for g, w in enumerate(POOL_WINDOWS):
            while width < w:
                run = run + pltpu.roll(run, shift=width, axis=0)
                width *= 2
            sl = slice(g * cg, (g + 1) * cg)
            pooled.append(run[HALO:, sl] / float(w) - u[:, sl])
    else:
        a = a_ref[...]
        seqs, ext_len, _ = uext_ref.shape
        n_new = ext_len - POOL_HIST
        pooled = []
        for g, w in enumerate(POOL_WINDOWS):
            sl = slice(g * cg, (g + 1) * cg)
            win = uext_ref[:, pl.ds(POOL_HIST, n_new), sl]
            for dlt in range(1, w):
                win = win + uext_ref[:, pl.ds(POOL_HIST - dlt, n_new), sl]
            tokv = uext_ref[:, pl.ds(POOL_HIST, n_new), sl]
            pooled.append((win / float(w) - tokv).reshape(seqs * n_new, cg))

    b = jnp.concatenate([_dot(pooled[g].astype(BF16), wp_ref[g]) for g in range(len(POOL_WINDOWS))], axis=-1)
    b = b * ps_ref[...]
    merged = sga_ref[...].astype(F32) * a + sgb_ref[...].astype(F32) * b
    x1 = x_ref[...] + _dot(merged.astype(BF16), wo_ref[...])
    x1_ref[...] = x1

    h = _rms(x1, gf_ref[...])
    h_hi = h.astype(BF16)
    h_lo = (h - h_hi.astype(F32)).astype(BF16)
    logits = (_dot(h_hi, wrh_ref[...]) + _dot(h_lo, wrh_ref[...]) + _dot(h_hi, wrl_ref[...])) + br_ref[...]
    e1, e2, w1, w2 = _route(logits)
    lane = lax.broadcasted_iota(jnp.int32, logits.shape, 1)
    route_ref[...] = jnp.where(lane == 0, e1, jnp.where(lane == 1, e2, jnp.where(lane == 2, w1, jnp.where(lane == 3, w2, 0.0))))


def _merge(x, a, u_parts, sga, sgb, wp_bf, ps, wo_bf, gf, wr_hi, wr_lo, br, *, mode, tm, seq, n_total, x1_buf=None):
    n, dm = x.shape
    pw = wp_bf.shape[0] * wp_bf.shape[1]
    n_heads = dm // HEAD
    assert n % tm == 0 and n_total % tm == 0
    n_own = n // tm
    own = lambda i: jnp.minimum(i, n_own - 1)
    row = lambda c: pl.BlockSpec((tm, c), lambda i: (own(i), 0))
    const = lambda shape: pl.BlockSpec(shape, lambda i: (0,) * len(shape))
    extra_specs, extra_args, aliases = [], [], {}
    n_steps = n_own
    if mode == "prompt":
        assert seq % tm == 0 and tm % HALO == 0
        u, u_meta = u_parts
        hb = tm // HALO
        a_spec = pl.BlockSpec((n_heads, tm, HEAD), lambda i: (0, own(i), 0))
        u_specs = [row(pw), pl.BlockSpec((HALO, pw), lambda i: (jnp.maximum(own(i) * hb - 1, 0), 0)), const((HALO, pw))]
        u_args = [u, u, u_meta]
        tiles_per_seq = seq // tm
        x1_spec = pl.BlockSpec((tm, dm), lambda i: (i, 0))
        n_steps = n_total // tm
    else:
        (u_ext,) = u_parts
        n_new = u_ext.shape[1] - POOL_HIST
        assert tm % n_new == 0
        a_spec = row(dm)
        u_specs = [pl.BlockSpec((tm // n_new, u_ext.shape[1], pw), lambda i: (i, 0, 0))]
        u_args = [u_ext]
        tiles_per_seq = 1
        first_tile = (n_total - n) // tm
        x1_spec = pl.BlockSpec((tm, dm), lambda i: (i + first_tile, 0))
        extra_specs, extra_args = [pl.BlockSpec(memory_space=pl.ANY)], [x1_buf]
        aliases = {12 + len(u_specs) - 1: 0}
    in_specs = [row(dm), a_spec, *u_specs, row(dm), row(dm),
                const(wp_bf.shape), const((1, dm)), const((dm, dm)), const((1, dm)),
                const((dm, LANES)), const((dm, LANES)), const((1, LANES)), *extra_specs]
    args = [x, a, *u_args, sga, sgb, wp_bf, ps, wo_bf, gf, wr_hi, wr_lo, br, *extra_args]
    if aliases:
        assert args[next(iter(aliases))] is x1_buf
    return pl.pallas_call(
        functools.partial(_merge_kernel, n_own=n_own, mode=mode, tm=tm, dm=dm, pw=pw, n_heads=n_heads,
                          tiles_per_seq=tiles_per_seq),
        grid=(n_steps,),
        in_specs=in_specs,
        out_specs=[x1_spec, row(LANES)],
        out_shape=[jax.ShapeDtypeStruct((n_total, dm), F32), jax.ShapeDtypeStruct((n, LANES), F32)],
        input_output_aliases=aliases,
        compiler_params=pltpu.CompilerParams(dimension_semantics=("arbitrary",), vmem_limit_bytes=VMEM_LIMIT),
        name="merge_" + mode,
    )(*args)


def _moe_kernel(ea_ref, eb_ref, nv_ref, pure_ref, tok_ref, tokn_ref, w_ref, x_hbm,
                wga_ref, wua_ref, wda_ref, wgb_ref, wub_ref, wdb_ref, gf_ref, gl_ref,
                op_hbm, os_hbm, xbuf, ybuf, gsem, ssem, *, tm, n_prompt):
    del ea_ref, eb_ref
    t = pl.program_id(0)
    n_t = pl.num_programs(0)
    slot = t % 2

    def gather(tok_smem, r, sl):
        return pltpu.make_async_copy(x_hbm.at[pl.ds(tok_smem[0, 0, r], 1)], xbuf.at[sl, pl.ds(r, 1)], gsem.at[sl])

    def scatter(out_hbm, r, sl, dst):
        return pltpu.make_async_copy(ybuf.at[sl, pl.ds(r, 1)], out_hbm.at[pl.ds(dst, 1)], ssem.at[sl])

    def counted(n, fn):
        def body(r, carry):
            fn(r)
            return carry
        lax.fori_loop(0, n, body, 0)

    def wait_scatters(sl, n):
        @pl.when(n == tm)
        def _():
            for r in range(tm):
                scatter(op_hbm, r, sl, 0).wait()

        @pl.when(n < tm)
        def _():
            counted(n, lambda r: scatter(op_hbm, r, sl, 0).wait())

    def start_scatters(sl, n, pure):
        fast = (n == tm) & (pure > 0)

        @pl.when(fast)
        def _():
            for r in range(tm):
                scatter(op_hbm, r, sl, tok_ref[0, 0, r]).start(priority=r % 2)

        @pl.when(jnp.logical_not(fast))
        def _():
            def one(r):
                tok = tok_ref[0, 0, r]

                @pl.when(tok < n_prompt)
                def _():
                    scatter(op_hbm, r, sl, tok).start()

                @pl.when(tok >= n_prompt)
                def _():
                    scatter(os_hbm, r, sl, tok - n_prompt).start()
            counted(n, one)

    nv = nv_ref[t]

    @pl.when((t == 0) & (nv > 0))
    def _():
        for r in range(tm):
            gather(tok_ref, r, 0).start(priority=r % 2)

    @pl.when(t >= 2)
    def _():
        wait_scatters(slot, nv_ref[jnp.maximum(t - 2, 0)])

    @pl.when((nv == 0) & (t >= 1) & (nv_ref[jnp.maximum(t - 1, 0)] > 0))
    def _():
        for r in range(tm):
            gather(tok_ref, r, slot).wait()

    @pl.when(nv > 0)
    def _():
        for r in range(tm):
            gather(tok_ref, r, slot).wait()
        for r in range(tm):
            gather(tokn_ref, r, 1 - slot).start(priority=r % 2)
        x = xbuf[slot]
        h = _rms(x, gf_ref[...]).astype(BF16)

        def expert(wg_ref, wu_ref, wd_ref):
            gate = _dot(h, wg_ref[...])
            up = _dot(h, wu_ref[...])
            act = gate * _sigmoid(gate) * up
            return _dot(act.astype(BF16), wd_ref[...])

        y = w_ref[:, 0:1] * expert(wga_ref, wua_ref, wda_ref)
        y = y + w_ref[:, 1:2] * expert(wgb_ref, wub_ref, wdb_ref)
        ybuf[slot] = _rms(x + y, gl_ref[...])
        start_scatters(slot, nv, pure_ref[t])

    @pl.when(t == n_t - 1)
    def _():
        wait_scatters(slot, nv)

        @pl.when(t >= 1)
        def _():
            wait_scatters(1 - slot, nv_ref[jnp.maximum(t - 1, 0)])


def _moe(x1, n_prompt, tile_ea, tile_eb, tile_nv, tile_pure, tok_tiles, w_tiles, wg_bf, wu_bf, wd_bf, gf, gl, *, tm):
    n_tok, dm = x1.shape
    n_steps = tile_ea.shape[0]
    de = wg_bf.shape[2]
    tok_spec = lambda off: pl.BlockSpec((1, 1, tm), lambda t, *_: (jnp.minimum(t + off, n_steps - 1), 0, 0),
                                        memory_space=pltpu.SMEM)
    wspec = lambda shape, which: pl.BlockSpec((None,) + shape, (lambda t, ea, eb, *_: (ea[t], 0, 0)) if which == 0
                                              else (lambda t, ea, eb, *_: (eb[t], 0, 0)))
    vec = pl.BlockSpec((1, dm), lambda t, *_: (0, 0))
    any_spec = pl.BlockSpec(memory_space=pl.ANY)
    grid_spec = pltpu.PrefetchScalarGridSpec(
        num_scalar_prefetch=4,
        grid=(n_steps,),
        in_specs=[tok_spec(0), tok_spec(1),
                  pl.BlockSpec((tm, 2), lambda t, *_: (t, 0)),
                  any_spec,
                  wspec((dm, de), 0), wspec((dm, de), 0), wspec((de, dm), 0),
                  wspec((dm, de), 1), wspec((dm, de), 1), wspec((de, dm), 1),
                  vec, vec],
        out_specs=[any_spec, any_spec],
        scratch_shapes=[pltpu.VMEM((2, tm, dm), F32), pltpu.VMEM((2, tm, dm), F32),
                        pltpu.SemaphoreType.DMA((2,)), pltpu.SemaphoreType.DMA((2,))],
    )
    return pl.pallas_call(
        functools.partial(_moe_kernel, tm=tm, n_prompt=n_prompt),
        grid_spec=grid_spec,
        out_shape=[jax.ShapeDtypeStruct((n_prompt, dm), F32), jax.ShapeDtypeStruct((n_tok - n_prompt, dm), F32)],
        compiler_params=pltpu.CompilerParams(dimension_semantics=("arbitrary",), vmem_limit_bytes=VMEM_LIMIT),
        name="moe",
    )(tile_ea, tile_eb, tile_nv, tile_pure, tok_tiles, tok_tiles, w_tiles, x1,
      wg_bf, wu_bf, wd_bf, wg_bf, wu_bf, wd_bf, gf, gl)


def _moe_schedule(route, n_prompt, *, tm):
    n = route.shape[0]
    e1 = route[:, 0].astype(jnp.int32)
    e2 = route[:, 1].astype(jnp.int32)
    swap = e2 < e1
    ea = jnp.where(swap, e2, e1)
    eb = jnp.where(swap, e1, e2)
    wa = jnp.where(swap, route[:, 3], route[:, 2])
    wb = jnp.where(swap, route[:, 2], route[:, 3])
    grp = ea // EXPERTS_PER_GROUP
    la = ea % EXPERTS_PER_GROUP
    lb = eb % EXPERTS_PER_GROUP
    cls = grp * N_PAIRS + la * (2 * EXPERTS_PER_GROUP - 1 - la) // 2 + (lb - la - 1)
    cls_sorted, order = lax.sort_key_val(cls, jnp.arange(n, dtype=jnp.int32))
    classes = jnp.arange(N_CLASSES, dtype=jnp.int32)
    row_end = jnp.sum((cls_sorted[None, :] <= classes[:, None]).astype(jnp.int32), axis=1)
    row_start = jnp.concatenate([jnp.zeros((1,), jnp.int32), row_end[:-1]])
    counts = row_end - row_start
    tiles_per = (counts + tm - 1) // tm
    tile_end = jnp.cumsum(tiles_per)
    tile_start = tile_end - tiles_per
    n_tiles = n // tm + N_CLASSES + 1
    t = jnp.arange(n_tiles, dtype=jnp.int32)
    c_t = jnp.minimum(jnp.sum((tile_end[None, :] <= t[:, None]).astype(jnp.int32), axis=1), N_CLASSES - 1)
    active = t < tile_end[-1]
    k_t = t - tile_start[c_t]
    left = jnp.where(active, counts[c_t] - k_t * tm, 0)
    nvalid = jnp.clip(left, 0, tm).astype(jnp.int32)
    last_c = c_t[jnp.maximum(tile_end[-1] - 1, 0)]
    c_eff = jnp.where(active, c_t, last_c)
    g_c = c_eff // N_PAIRS
    pair = c_eff % N_PAIRS
    la_tab, lb_tab = zip(*[(i, j) for i in range(EXPERTS_PER_GROUP) for j in range(i + 1, EXPERTS_PER_GROUP)])
    la_t = jnp.asarray(la_tab, jnp.int32)[pair]
    lb_t = jnp.asarray(lb_tab, jnp.int32)[pair]
    tile_ea = (g_c * EXPERTS_PER_GROUP + la_t).astype(jnp.int32)
    tile_eb = (g_c * EXPERTS_PER_GROUP + lb_t).astype(jnp.int32)
    r = jnp.arange(tm, dtype=jnp.int32)
    idx = row_start[c_t][:, None] + k_t[:, None] * tm + r[None, :]
    valid = r[None, :] < nvalid[:, None]
    tok = jnp.where(valid, order[jnp.clip(idx, 0, n - 1)], 0).astype(jnp.int32)
    w_tiles = jnp.stack([jnp.where(valid, wa[tok], 0.0), jnp.where(valid, wb[tok], 0.0)], axis=-1)
    pure = jnp.all(tok < n_prompt, axis=1).astype(jnp.int32)
    return tile_ea, tile_eb, nvalid, pure, tok.reshape(n_tiles, 1, tm), w_tiles.reshape(n_tiles * tm, 2)


def kernel(x_prompt, x_sample, cache_k, cache_v, state_pool, page_table, meta_tokens, norm_mix_g, w_in,
           lambda_q1, lambda_k1, lambda_q2, lambda_k2, subln_g, w_pool, pool_scale, w_out, norm_ffn_g,
           w_group, b_group, w_router, b_router, w_gate, w_up, w_down, norm_final_g):
    batch, seq, dm = x_prompt.shape
    dec_batch, dec_seq, _ = x_sample.shape
    depth = w_in.shape[0]
    assert depth == 1, "single-layer step"
    n_heads = dm // HEAD
    page = cache_k.shape[2]
    assert page_table.shape[1] * page >= POOL_HIST and N_META >= POOL_HIST + 1
    assert N_META <= HALO and w_group.shape[-1] == N_GROUPS and w_router.shape[-1] == EXPERTS_PER_GROUP
    lam_init = 0.8 - 0.6 * math.exp(-0.3 * 0)
    n_p, n_s = batch * seq, dec_batch * dec_seq
    n_tok = n_p + n_s
    pw = w_pool.shape[1] * w_pool.shape[2]

    w_in_bf = w_in[0].astype(BF16)
    g_mix = norm_mix_g[0].reshape(1, dm)
    lam_vecs = [v[0].reshape(1, DK) for v in (lambda_q1, lambda_k1, lambda_q2, lambda_k2)]
    sg = subln_g[0].reshape(1, HEAD)
    wp_bf = w_pool[0].astype(BF16)
    ps = pool_scale[0].reshape(1, dm)
    wo_bf = w_out[0].astype(BF16)
    g_ffn = norm_ffn_g[0].reshape(1, dm)
    g_fin = norm_final_g.reshape(1, dm)
    n_log = N_GROUPS + N_GROUPS * EXPERTS_PER_GROUP
    wr = jnp.concatenate([w_group[0], jnp.moveaxis(w_router[0], 0, 1).reshape(dm, -1)], axis=1)
    wr = jnp.pad(wr, ((0, 0), (0, LANES - n_log)))
    wr_hi = wr.astype(BF16)
    wr_lo = (wr - wr_hi.astype(F32)).astype(BF16)
    br = jnp.pad(jnp.concatenate([b_group[0], b_router[0].reshape(-1)]), (0, LANES - n_log)).reshape(1, LANES)
    wg_bf, wu_bf, wd_bf = w_gate[0].astype(BF16), w_up[0].astype(BF16), w_down[0].astype(BF16)

    xp = x_prompt.reshape(n_p, dm)
    xs = x_sample.reshape(n_s, dm)

    k_m, v_m, u_m, kb_m, vb_m = _inproj(meta_tokens, g_mix, w_in_bf, mode="meta", tm=N_META)
    k_ext, v_ext, u_p, sga_p, sgb_p, q_hm, kb_hm, vb_hm = _inproj(xp, g_mix, w_in_bf, mode="prompt", tm=TM_PROJ,
                                                                  seq=seq, meta_kv=(k_m, v_m))
    k_s, v_s, u_s, sga_s, sgb_s, q_s = _inproj(xs, g_mix, w_in_bf, mode="sample", tm=TM_PROJ)

    pad_meta = lambda t: jnp.pad(t, ((0, 0), (0, LANES - N_META), (0, 0)))
    a_hm = _prompt_attention(q_hm, kb_hm, vb_hm, pad_meta(kb_m), pad_meta(vb_m), lam_vecs, sg,
                             batch=batch, seq=seq, lam_init=lam_init)
    a_s = _sample_attention(page_table, cache_k, cache_v, q_s, k_s, v_s, lam_vecs, sg, lam_init=lam_init)

    u_halo = jnp.pad(u_m, ((HALO - N_META, 0), (0, 0)))
    x1, route_p = _merge(xp, a_hm, (u_p, u_halo), sga_p, sgb_p, wp_bf, ps, wo_bf, g_ffn, wr_hi, wr_lo, br,
                         mode="prompt", tm=TM_PROJ, seq=seq, n_total=n_tok)
    u_ext_s = jnp.concatenate([state_pool[0], u_s.reshape(dec_batch, dec_seq, pw)], axis=1)
    x1, route_s = _merge(xs, a_s, (u_ext_s,), sga_s, sgb_s, wp_bf, ps, wo_bf, g_ffn, wr_hi, wr_lo, br,
                         mode="sample", tm=TM_PROJ, seq=dec_seq, n_total=n_tok, x1_buf=x1)

    route = jnp.concatenate([route_p[:, :4], route_s[:, :4]], axis=0)
    tile_ea, tile_eb, tile_nv, tile_pure, tok_tiles, w_tiles = _moe_schedule(route, n_p, tm=TM_MOE)
    y_p, y_s = _moe(x1, n_p, tile_ea, tile_eb, tile_nv, tile_pure, tok_tiles, w_tiles, wg_bf, wu_bf, wd_bf,
                    g_ffn, g_fin, tm=TM_MOE)

    t_ext = N_META + seq
    new_k_p = k_ext.reshape(1, batch, t_ext, n_heads, HEAD)
    new_v_p = v_ext.reshape(1, batch, t_ext, n_heads, HEAD)
    new_pool_p = u_p.reshape(batch, seq, pw)[:, seq - POOL_HIST:][None]
    new_k_s = k_s.reshape(1, dec_batch, dec_seq, n_heads, HEAD)
    new_v_s = v_s.reshape(1, dec_batch, dec_seq, n_heads, HEAD)
    new_pool_s = u_ext_s[:, u_ext_s.shape[1] - POOL_HIST:][None]
    return (y_p.reshape(batch, seq, dm), y_s.reshape(dec_batch, dec_seq, dm),
            new_k_p, new_v_p, new_pool_p, new_k_s, new_v_s, new_pool_s)
```
